```python
import math
import jax, jax.numpy as jnp
from jax import lax
import numpy as np

D_MODEL = 1024
BATCH = 8
SEQ = 4096
DEPTH = 1

MEM_LEN = 256
HEAD_DIM = 64
LRU_WIDTH = 512
LRU_BLOCKS = 8
LRU_BLOCK = LRU_WIDTH // LRU_BLOCKS
CONV_WIDTH = 4
LRU_C = 8.0
SWA_Q_HEADS = 4
SWA_KV_HEADS = 2
SWA_GROUP = SWA_Q_HEADS // SWA_KV_HEADS
SWA_WIDTH = SWA_Q_HEADS * HEAD_DIM
SWA_KV_WIDTH = SWA_KV_HEADS * HEAD_DIM
WINDOW = 128
BLOCK = 128
XATTN_HEADS = 4
XATTN_WIDTH = XATTN_HEADS * HEAD_DIM
D_MIX = LRU_WIDTH + SWA_WIDTH + XATTN_WIDTH
IN_SPLITS = (LRU_WIDTH, LRU_WIDTH, SWA_WIDTH, SWA_KV_WIDTH, SWA_KV_WIDTH, SWA_WIDTH, XATTN_WIDTH, XATTN_WIDTH)
D_IN = sum(IN_SPLITS)
ROPE_THETA = 500000.0
ROPE_DIM = HEAD_DIM // 4
EPS = 1e-6
NEG_INF = -1e30

kernel_name = "hymba_style_rglru_swa_sink_memxattn"


def _rmsnorm(x, g):
    xf = x.astype(jnp.float32)
    y = xf * lax.rsqrt(jnp.mean(xf * xf, axis=-1, keepdims=True) + EPS)
    return (y * g.astype(jnp.float32)).astype(x.dtype)


def _rope_tables(seq):
    pos = jnp.arange(seq, dtype=jnp.float32)
    inv_freq = ROPE_THETA ** (-(jnp.arange(0, ROPE_DIM, 2, dtype=jnp.float32) / ROPE_DIM))
    ang = pos[:, None] * inv_freq[None, :]
    return jnp.cos(ang), jnp.sin(ang)


def _partial_rope(t, cos, sin):
    tf = t.astype(jnp.float32)
    half = ROPE_DIM // 2
    x1, x2, rest = tf[..., :half], tf[..., half:ROPE_DIM], tf[..., ROPE_DIM:]
    c, s = cos[None, :, None, :], sin[None, :, None, :]
    return jnp.concatenate([x1 * c - x2 * s, x2 * c + x1 * s, rest], axis=-1).astype(t.dtype)


def _rg_lru(u, conv_w, conv_b, w_rg, b_rg, w_ig, b_ig, lam):
    B, S, W = u.shape
    xc = lax.conv_general_dilated(
        u, conv_w[:, None, :].astype(u.dtype), window_strides=(1,),
        padding=[(CONV_WIDTH - 1, 0)], dimension_numbers=("NWC", "WIO", "NWC"),
        feature_group_count=W) + conv_b.astype(u.dtype)
    xf = xc.astype(jnp.float32)
    xblk = xf.reshape(B, S, LRU_BLOCKS, LRU_BLOCK)
    r = jax.nn.sigmoid(jnp.einsum("bsnd,nde->bsne", xblk, w_rg.astype(jnp.float32)).reshape(B, S, W)
                       + b_rg.astype(jnp.float32))
    i = jax.nn.sigmoid(jnp.einsum("bsnd,nde->bsne", xblk, w_ig.astype(jnp.float32)).reshape(B, S, W)
                       + b_ig.astype(jnp.float32))
    log_a = -LRU_C * r * jax.nn.softplus(-lam.astype(jnp.float32))
    a = jnp.exp(log_a)
    b = jnp.sqrt(-jnp.expm1(2.0 * log_a)) * (i * xf)

    def combine(c1, c2):
        a1, b1 = c1
        a2, b2 = c2
        return a1 * a2, a2 * b1 + b2

    _, h = lax.associative_scan(combine, (a, b), axis=1)
    return h.astype(u.dtype)


def _band(t):
    B, S, H, D = t.shape
    tb = t.reshape(B, S // BLOCK, BLOCK, H, D)
    prev = jnp.pad(tb, ((0, 0), (1, 0), (0, 0), (0, 0), (0, 0)))[:, :-1]
    return jnp.concatenate([prev, tb], axis=2)


def _swa_sink_attention(q, k, v, sinks):
    B, S, Hq, D = q.shape
    nb = S // BLOCK
    qb = q.reshape(B, nb, BLOCK, SWA_KV_HEADS, SWA_GROUP, D)
    kb, vb = _band(k), _band(v)
    s = jnp.einsum("bnqhgd,bnkhd->bnhgqk", qb, kb,
                   preferred_element_type=jnp.float32) * (1.0 / math.sqrt(D))
    qi = jnp.arange(BLOCK)[:, None]
    kj = jnp.arange(2 * BLOCK)[None, :]
    rel = qi + BLOCK - kj
    band_mask = (rel >= 0) & (rel < WINDOW)
    blk = jnp.arange(nb)[:, None, None]
    mask = band_mask[None] & ((blk > 0) | (kj >= BLOCK)[None])
    s = jnp.where(mask[None, :, None, None], s, NEG_INF)
    sink = sinks.astype(jnp.float32).reshape(SWA_KV_HEADS, SWA_GROUP)[None, None, :, :, None, None]
    m = jnp.maximum(jnp.max(s, axis=-1, keepdims=True), sink)
    p = jnp.exp(s - m)
    denom = jnp.sum(p, axis=-1, keepdims=True) + jnp.exp(sink - m)
    o = jnp.einsum("bnhgqk,bnkhd->bnqhgd", (p / denom).astype(v.dtype), vb)
    return o.reshape(B, S, Hq * D)


def _memory_attention(q, km, vm):
    B, S, H, D = q.shape
    s = jnp.einsum("bshd,bmhd->bhsm", q, km, preferred_element_type=jnp.float32) * (1.0 / math.sqrt(D))
    p = jax.nn.softmax(s, axis=-1)
    o = jnp.einsum("bhsm,bmhd->bshd", p.astype(vm.dtype), vm)
    return o.reshape(B, S, H * D)


def setup_inputs(seed: int = 0) -> dict:
    key = jax.random.key(seed)
    ks = jax.random.split(key, 20)
    f32 = jnp.float32
    nrm = lambda k, shape, scale: jax.random.normal(k, shape, f32) * scale
    x = jax.random.normal(ks[0], (BATCH, SEQ, D_MODEL), f32)
    mem = jax.random.normal(ks[1], (BATCH, MEM_LEN, D_MODEL), f32)
    u = jax.random.uniform(ks[11], (DEPTH, LRU_WIDTH), f32, 0.9, 0.999) ** (1.0 / LRU_C)
    lru_lambda = jnp.log(u) - jnp.log1p(-u)
    return {
        "x": x,
        "mem": mem,
        "norm_g": 1.0 + nrm(ks[2], (DEPTH, D_MODEL), 0.02),
        "mem_norm_g": 1.0 + nrm(ks[3], (DEPTH, D_MODEL), 0.02),
        "w_in": nrm(ks[4], (DEPTH, D_MODEL, D_IN), D_MODEL ** -0.5),
        "conv_w": nrm(ks[5], (DEPTH, CONV_WIDTH, LRU_WIDTH), CONV_WIDTH ** -0.5),
        "conv_b": nrm(ks[6], (DEPTH, LRU_WIDTH), 0.01),
        "w_rg": nrm(ks[7], (DEPTH, LRU_BLOCKS, LRU_BLOCK, LRU_BLOCK), LRU_BLOCK ** -0.5),
        "b_rg": nrm(ks[8], (DEPTH, LRU_WIDTH), 0.01),
        "w_ig": nrm(ks[9], (DEPTH, LRU_BLOCKS, LRU_BLOCK, LRU_BLOCK), LRU_BLOCK ** -0.5),
        "b_ig": nrm(ks[10], (DEPTH, LRU_WIDTH), 0.01),
        "lru_lambda": lru_lambda,
        "q_norm_g": 1.0 + nrm(ks[12], (DEPTH, HEAD_DIM), 0.02),
        "k_norm_g": 1.0 + nrm(ks[13], (DEPTH, HEAD_DIM), 0.02),
        "sinks": nrm(ks[14], (DEPTH, SWA_Q_HEADS), 0.5),
        "w_mem_kv": nrm(ks[15], (DEPTH, D_MODEL, 2 * XATTN_WIDTH), D_MODEL ** -0.5),
        "xq_norm_g": 1.0 + nrm(ks[16], (DEPTH, HEAD_DIM), 0.02),
        "xk_norm_g": 1.0 + nrm(ks[17], (DEPTH, HEAD_DIM), 0.02),
        "out_norm_g": 1.0 + nrm(ks[18], (DEPTH, D_MIX), 0.02),
        "w_out": nrm(ks[19], (DEPTH, D_MIX, D_MODEL), D_MIX ** -0.5),
    }


def reference(x, mem, norm_g, mem_norm_g, w_in, conv_w, conv_b, w_rg, b_rg, w_ig, b_ig, lru_lambda,
              q_norm_g, k_norm_g, sinks, w_mem_kv, xq_norm_g, xk_norm_g, out_norm_g, w_out):
    B, S, _ = x.shape
    M = mem.shape[1]
    cos, sin = _rope_tables(S)
    split_idx = list(np.cumsum(IN_SPLITS)[:-1])
    out_idx = [LRU_WIDTH, LRU_WIDTH + SWA_WIDTH]
    h = x
    for l in range(DEPTH):
        xn = _rmsnorm(h, norm_g[l])
        proj = jnp.einsum("bsd,de->bse", xn, w_in[l])
        (lru_x, lru_gate, sq, sk, sv, swa_gate, xq, x_gate) = jnp.split(proj, split_idx, axis=-1)

        y_a = _rg_lru(lru_x, conv_w[l], conv_b[l], w_rg[l], b_rg[l], w_ig[l], b_ig[l], lru_lambda[l])

        q = _partial_rope(_rmsnorm(sq.reshape(B, S, SWA_Q_HEADS, HEAD_DIM), q_norm_g[l]), cos, sin)
        k = _partial_rope(_rmsnorm(sk.reshape(B, S, SWA_KV_HEADS, HEAD_DIM), k_norm_g[l]), cos, sin)
        v = sv.reshape(B, S, SWA_KV_HEADS, HEAD_DIM)
        y_b = _swa_sink_attention(q, k, v, sinks[l])

        mn = _rmsnorm(mem, mem_norm_g[l])
        mkv = jnp.einsum("bmd,de->bme", mn, w_mem_kv[l])
        km, vm = jnp.split(mkv, 2, axis=-1)
        km = _rmsnorm(km.reshape(B, M, XATTN_HEADS, HEAD_DIM), xk_norm_g[l])
        vm = vm.reshape(B, M, XATTN_HEADS, HEAD_DIM)
        qx = _rmsnorm(xq.reshape(B, S, XATTN_HEADS, HEAD_DIM), xq_norm_g[l])
        y_c = _memory_attention(qx, km, vm)

        g_a, g_b, g_c = jnp.split(out_norm_g[l], out_idx)
        y = jnp.concatenate([
            _rmsnorm(y_a, g_a) * jax.nn.silu(lru_gate),
            _rmsnorm(y_b, g_b) * jax.nn.silu(swa_gate),
            _rmsnorm(y_c, g_c) * jax.nn.silu(x_gate),
        ], axis=-1)
        h = h + jnp.einsum("bse,ed->bsd", y, w_out[l]).astype(h.dtype)
    return h
```

```python
import functools
import math

import jax
import jax.numpy as jnp
import numpy as np
from jax import lax
from jax.experimental import pallas as pl
from jax.experimental.pallas import tpu as pltpu

D_MODEL = 1024
MEM_LEN = 256
HEAD_DIM = 64
LRU_WIDTH = 512
LRU_BLOCKS = 8
LRU_BLOCK = LRU_WIDTH // LRU_BLOCKS
CONV_WIDTH = 4
LRU_C = 8.0
SWA_Q_HEADS = 4
SWA_KV_HEADS = 2
SWA_WIDTH = SWA_Q_HEADS * HEAD_DIM
SWA_KV_WIDTH = SWA_KV_HEADS * HEAD_DIM
WINDOW = 128
BLOCK = 128
XATTN_HEADS = 4
XATTN_WIDTH = XATTN_HEADS * HEAD_DIM
D_MIX = LRU_WIDTH + SWA_WIDTH + XATTN_WIDTH
ROPE_THETA = 500000.0
ROPE_DIM = HEAD_DIM // 4
EPS = 1e-6
NEG_INF = -1e30

LANES = 128
SUBLANES = 8
MXU_DIM = 256
VMEM_LIMIT_BYTES = 56 * 1024 * 1024

_P_GATE_A = 0
_P_Q = _P_GATE_A + LRU_WIDTH
_P_K = _P_Q + SWA_WIDTH
_P_V = _P_K + SWA_KV_WIDTH
_P_GATE_B = _P_V + SWA_KV_WIDTH
_P_XQ = _P_GATE_B + SWA_WIDTH
_P_GATE_C = _P_XQ + XATTN_WIDTH
_P_END = _P_GATE_C + XATTN_WIDTH

_HEAD_SLOTS = ((0, 0), (1, 0), (0, 1), (1, 1))
_HEAD_PERM = (0, 2, 1, 3)

_BF16 = jnp.bfloat16
_F32 = jnp.float32


def _dot(a, b):
    return jnp.dot(a, b, preferred_element_type=_F32)


def _dot_nt(a, b):
    return lax.dot_general(a, b, (((1,), (1,)), ((), ())), preferred_element_type=_F32)


def _sigmoid(z):
    return 1.0 / (1.0 + jnp.exp(-z))


def _row_rms_scale(v):
    return lax.rsqrt(jnp.mean(v * v, axis=-1, keepdims=True) + EPS)


def _head_rms_scale(v, head_mean_ref):
    sq = (v * v).astype(_BF16)
    return lax.rsqrt(_dot(sq, head_mean_ref[...]) + EPS)


def _rope(v, cos, sin):
    half = ROPE_DIM // 2
    lane = lax.broadcasted_iota(jnp.int32, v.shape, 1) % HEAD_DIM
    partner = jnp.where(lane < half, pltpu.roll(v, LANES - half, axis=1), pltpu.roll(v, half, axis=1))
    return v * cos + partner * sin


def _mem_kv_kernel(mem_ref, g_ref, w_ref, kg_ref, head_mean_ref, km_ref, vm_ref):
    m = mem_ref[0]
    mn = (m * _row_rms_scale(m) * g_ref[...]).astype(_BF16)
    kv = _dot(mn, w_ref[...])
    k = kv[:, :XATTN_WIDTH]
    k = k * _head_rms_scale(k, head_mean_ref) * kg_ref[...]
    km_ref[0] = k.astype(_BF16)
    vm_ref[0] = kv[:, XATTN_WIDTH:].astype(_BF16)


def _layer_kernel(
    sinks_ref, x_ref, km_ref, vm_ref, cos_ref, sin_ref, norm_g_ref, w_in_ref, conv_w_ref, conv_b_ref,
    w_gate_ref, b_rg_ref, b_ig_ref, lam_ref, qg_ref, kg_ref, xqg_ref, head_mean_ref, out_g_ref, w_out_ref,
    o_ref,
    proj_scr, conv_scr, a_scr, b_scr, h_scr, carry_scr, k_scr, v_scr, y_scr,
    *, tile,
):
    s = pl.program_id(1)
    n_blocks = tile // BLOCK

    @pl.when(s == 0)
    def _reset_sequence_state():
        conv_scr[0:SUBLANES, :] = jnp.zeros((SUBLANES, LRU_WIDTH), _F32)
        carry_scr[...] = jnp.zeros_like(carry_scr)
        k_scr[tile:tile + BLOCK, :] = jnp.zeros((BLOCK, LANES), _BF16)
        v_scr[tile:tile + BLOCK, :] = jnp.zeros((BLOCK, LANES), _BF16)

    k_scr[0:BLOCK, :] = k_scr[tile:tile + BLOCK, :]
    v_scr[0:BLOCK, :] = v_scr[tile:tile + BLOCK, :]

    x = x_ref[0]
    xn = (x * _row_rms_scale(x) * norm_g_ref[...]).astype(_BF16)
    conv_scr[SUBLANES:SUBLANES + tile, :] = _dot(xn, w_in_ref[:, 0:LRU_WIDTH])
    proj_scr[...] = _dot(xn, w_in_ref[:, LRU_WIDTH:])

    xc = conv_b_ref[...]
    for j in range(CONV_WIDTH):
        off = SUBLANES - (CONV_WIDTH - 1) + j
        xc = xc + conv_w_ref[j:j + 1, :] * conv_scr[off:off + tile, :]
    conv_scr[0:SUBLANES, :] = conv_scr[tile:tile + SUBLANES, :]

    lam = -lam_ref[...]
    softplus = jnp.maximum(lam, 0.0) + jnp.log1p(jnp.exp(-jnp.abs(lam)))
    neg_c_softplus = -LRU_C * softplus
    xcb = xc.astype(_BF16)
    for j in range(LRU_WIDTH // MXU_DIM):
        cols = slice(j * MXU_DIM, (j + 1) * MXU_DIM)
        pre = _dot(xcb[:, cols], w_gate_ref[j])
        r = _sigmoid(pre[:, :MXU_DIM] + b_rg_ref[:, cols])
        i = _sigmoid(pre[:, MXU_DIM:] + b_ig_ref[:, cols])
        a = jnp.exp(r * neg_c_softplus[:, cols])
        a_scr[:, cols] = a
        b_scr[:, cols] = jnp.sqrt(1.0 - a * a) * (i * xc[:, cols])

    a3 = a_scr[...].reshape(tile // SUBLANES, SUBLANES, LRU_WIDTH)
    b3 = b_scr[...].reshape(tile // SUBLANES, SUBLANES, LRU_WIDTH)
    row = lax.broadcasted_iota(jnp.int32, a3.shape, 1)
    d = 1
    while d < SUBLANES:
        keep = row >= d
        a_prev = jnp.where(keep, pltpu.roll(a3, d, axis=1), 1.0)
        b_prev = jnp.where(keep, pltpu.roll(b3, d, axis=1), 0.0)
        b3 = b3 + a3 * b_prev
        a3 = a3 * a_prev
        d *= 2
    a_scr[...] = a3.reshape(tile, LRU_WIDTH)
    b_scr[...] = b3.reshape(tile, LRU_WIDTH)
    carry = carry_scr[...]
    for g in range(tile // SUBLANES):
        rows = slice(g * SUBLANES, (g + 1) * SUBLANES)
        h = b_scr[rows, :] + a_scr[rows, :] * carry
        h_scr[rows, :] = h
        carry = h[SUBLANES - 1:SUBLANES, :]
    carry_scr[...] = carry

    h = h_scr[...]
    gate_a = proj_scr[:, _P_GATE_A:_P_GATE_A + LRU_WIDTH]
    y_a = h * _row_rms_scale(h) * out_g_ref[:, 0:LRU_WIDTH] * (gate_a * _sigmoid(gate_a))
    y_scr[:, 0:LRU_WIDTH] = y_a.astype(_BF16)

    pos = pl.ds(pl.multiple_of(s * tile, tile), tile)
    cos = cos_ref[pos, :]
    sin = sin_ref[pos, :]
    inv_sqrt_d = 1.0 / math.sqrt(HEAD_DIM)

    q_groups = []
    for g in range(SWA_WIDTH // LANES):
        cols = slice(_P_Q + g * LANES, _P_Q + (g + 1) * LANES)
        qraw = proj_scr[:, cols]
        qn = qraw * _head_rms_scale(qraw, head_mean_ref) * qg_ref[...]
        q_groups.append(_rope(qn, cos, sin) * inv_sqrt_d)
    kraw = proj_scr[:, _P_K:_P_K + SWA_KV_WIDTH]
    kn = kraw * _head_rms_scale(kraw, head_mean_ref) * kg_ref[...]
    k_scr[BLOCK:BLOCK + tile, :] = _rope(kn, cos, sin).astype(_BF16)
    v_scr[BLOCK:BLOCK + tile, :] = proj_scr[:, _P_V:_P_V + SWA_KV_WIDTH].astype(_BF16)

    xq_groups = []
    for g in range(XATTN_WIDTH // LANES):
        cols = slice(_P_XQ + g * LANES, _P_XQ + (g + 1) * LANES)
        xraw = proj_scr[:, cols]
        xq_groups.append(xraw * _head_rms_scale(xraw, head_mean_ref) * xqg_ref[...] * inv_sqrt_d)

    lane = lax.broadcasted_iota(jnp.int32, (BLOCK, LANES), 1)
    low_half = lane < HEAD_DIM
    half_masks = (low_half, lane >= HEAD_DIM)
    qi =lax.broadcasted_iota(jnp.int32, (BLOCK, 2 * BLOCK), 0)
    kj = lax.broadcasted_iota(jnp.int32, (BLOCK, 2 * BLOCK), 1)
    rel = qi + BLOCK - kj
    band = (rel >= 0) & (rel < WINDOW)

    for blk in range(n_blocks):
        rows = slice(blk * BLOCK, (blk + 1) * BLOCK)
        if blk == 0:
            mask = band & (kj >= jnp.where(s > 0, 0, BLOCK))
        else:
            mask = band
        kk = k_scr[blk * BLOCK:(blk + 2) * BLOCK, :]
        vv = v_scr[blk * BLOCK:(blk + 2) * BLOCK, :]
        o_swa = [None, None]
        for head, (g, hi) in enumerate(_HEAD_SLOTS):
            qm = jnp.where(half_masks[hi], q_groups[g][rows, :], 0.0).astype(_BF16)
            sc = jnp.where(mask, _dot_nt(qm, kk), NEG_INF)
            sink = sinks_ref[head]
            m = jnp.maximum(jnp.max(sc, axis=-1, keepdims=True), sink)
            p = jnp.exp(sc - m)
            den = jnp.sum(p, axis=-1, keepdims=True) + jnp.exp(sink - m)
            o = _dot(p.astype(_BF16), vv) * (1.0 / den)
            o_swa[g] = o if o_swa[g] is None else jnp.where(low_half, o_swa[g], o)
        y_b = jnp.concatenate(o_swa, axis=-1)
        gate_b = proj_scr[rows, _P_GATE_B:_P_GATE_B + SWA_WIDTH]
        y_b = y_b * _row_rms_scale(y_b) * out_g_ref[:, LRU_WIDTH:LRU_WIDTH + SWA_WIDTH] * (gate_b * _sigmoid(gate_b))
        y_scr[rows, LRU_WIDTH:LRU_WIDTH + SWA_WIDTH] = y_b.astype(_BF16)

        o_mem = [None, None]
        for head, (g, hi) in enumerate(_HEAD_SLOTS):
            qm = jnp.where(half_masks[hi], xq_groups[g][rows, :], 0.0).astype(_BF16)
            sc = _dot_nt(qm, km_ref[0, :, g * LANES:(g + 1) * LANES])
            m = jnp.max(sc, axis=-1, keepdims=True)
            p = jnp.exp(sc - m)
            den = jnp.sum(p, axis=-1, keepdims=True)
            o = _dot(p.astype(_BF16), vm_ref[0, :, g * LANES:(g + 1) * LANES]) * (1.0 / den)
            o_mem[g] = o if o_mem[g] is None else jnp.where(low_half, o_mem[g], o)
        y_c = jnp.concatenate(o_mem, axis=-1)
        gate_c = proj_scr[rows, _P_GATE_C:_P_GATE_C + XATTN_WIDTH]
        y_c = y_c * _row_rms_scale(y_c) * out_g_ref[:, LRU_WIDTH + SWA_WIDTH:] * (gate_c * _sigmoid(gate_c))
        y_scr[rows, LRU_WIDTH + SWA_WIDTH:] = y_c.astype(_BF16)

    o_ref[0] = x + _dot(y_scr[...], w_out_ref[...])


def _permute_heads(w, axis):
    shape = w.shape
    split = shape[:axis] + (4, HEAD_DIM) + shape[axis + 1:]
    return jnp.take(w.reshape(split), jnp.array(_HEAD_PERM), axis=axis).reshape(shape)


def _rope_tables(seq):
    pos = jnp.arange(seq, dtype=_F32)
    inv_freq = ROPE_THETA ** (-(jnp.arange(0, ROPE_DIM, 2, dtype=_F32) / ROPE_DIM))
    ang = pos[:, None] * inv_freq[None, :]
    cos, sin = jnp.cos(ang), jnp.sin(ang)
    half = ROPE_DIM // 2
    ones = jnp.ones((seq, HEAD_DIM - ROPE_DIM), _F32)
    cos_head = jnp.concatenate([cos, cos, ones], axis=-1)
    sin_head = jnp.concatenate([-sin, sin, 0.0 * ones], axis=-1)
    reps = LANES // HEAD_DIM
    return jnp.tile(cos_head, (1, reps)), jnp.tile(sin_head, (1, reps))


def _const_spec(shape):
    return pl.BlockSpec(shape, lambda *_: (0,) * len(shape))


def _layer(x, km, vm, p, *, tile):
    B, S, _ = x.shape
    n_tiles = S // tile
    row = lambda v: v.reshape(1, -1).astype(_F32)

    lru_x, gate_a, sq, sk, sv, gate_b, xq, gate_c = jnp.split(
        p["w_in"], list(np.cumsum((LRU_WIDTH, LRU_WIDTH, SWA_WIDTH, SWA_KV_WIDTH, SWA_KV_WIDTH, SWA_WIDTH,
                                    XATTN_WIDTH))), axis=-1)
    w_in = jnp.concatenate(
        [lru_x, gate_a, _permute_heads(sq, 1), sk, sv, _permute_heads(gate_b, 1), _permute_heads(xq, 1),
         _permute_heads(gate_c, 1)], axis=-1).astype(_BF16)

    def block_diag(w):
        eye = jnp.eye(LRU_BLOCKS, dtype=w.dtype)
        return jnp.einsum("nde,nm->ndme", w, eye).reshape(LRU_WIDTH, LRU_WIDTH)

    w_r, w_i = block_diag(p["w_rg"]), block_diag(p["w_ig"])
    w_gate = jnp.stack([
        jnp.concatenate([w_r[c:c + MXU_DIM, c:c + MXU_DIM], w_i[c:c + MXU_DIM, c:c + MXU_DIM]], axis=-1)
        for c in range(0, LRU_WIDTH, MXU_DIM)]).astype(_BF16)

    g_a, g_b, g_c = jnp.split(p["out_norm_g"], [LRU_WIDTH, LRU_WIDTH + SWA_WIDTH])
    out_g = jnp.concatenate([g_a, _permute_heads(g_b, 0), _permute_heads(g_c, 0)])
    w_a, w_b, w_c = jnp.split(p["w_out"], [LRU_WIDTH, LRU_WIDTH + SWA_WIDTH], axis=0)
    w_out = jnp.concatenate([w_a, _permute_heads(w_b, 0), _permute_heads(w_c, 0)], axis=0).astype(_BF16)

    head_mean = jnp.kron(jnp.eye(LANES // HEAD_DIM, dtype=_F32),
                         jnp.full((HEAD_DIM, HEAD_DIM), 1.0 / HEAD_DIM, _F32)).astype(_BF16)
    cos, sin = _rope_tables(S)
    tile_heads = lambda g: jnp.tile(g.astype(_F32), LANES // HEAD_DIM).reshape(1, LANES)

    operands = [
        p["sinks"].astype(_F32),
        x, km, vm, cos, sin, row(p["norm_g"]), w_in, p["conv_w"].astype(_F32), row(p["conv_b"]),
        w_gate, row(p["b_rg"]), row(p["b_ig"]), row(p["lru_lambda"]),
        tile_heads(p["q_norm_g"]), tile_heads(p["k_norm_g"]), tile_heads(p["xq_norm_g"]),
        head_mean, row(out_g), w_out,
    ]
    in_specs = [
        pl.BlockSpec(memory_space=pltpu.SMEM),
        pl.BlockSpec((1, tile, D_MODEL), lambda b, s: (b, s, 0)),
        pl.BlockSpec((1, MEM_LEN, XATTN_WIDTH), lambda b, s: (b, 0, 0)),
        pl.BlockSpec((1, MEM_LEN, XATTN_WIDTH), lambda b, s: (b, 0, 0)),
    ] + [_const_spec(a.shape) for a in operands[4:]]

    return pl.pallas_call(
        functools.partial(_layer_kernel, tile=tile),
        out_shape=jax.ShapeDtypeStruct(x.shape, x.dtype),
        grid=(B, n_tiles),
        in_specs=in_specs,
        out_specs=pl.BlockSpec((1, tile, D_MODEL), lambda b, s: (b, s, 0)),
        scratch_shapes=[
            pltpu.VMEM((tile, _P_END), _F32),
            pltpu.VMEM((tile + SUBLANES, LRU_WIDTH), _F32),
            pltpu.VMEM((tile, LRU_WIDTH), _F32),
            pltpu.VMEM((tile, LRU_WIDTH), _F32),
            pltpu.VMEM((tile, LRU_WIDTH), _F32),
            pltpu.VMEM((1, LRU_WIDTH), _F32),
            pltpu.VMEM((tile + BLOCK, LANES), _BF16),
            pltpu.VMEM((tile + BLOCK, LANES), _BF16),
            pltpu.VMEM((tile, D_MIX), _BF16),
        ],
        compiler_params=pltpu.CompilerParams(
            dimension_semantics=("arbitrary", "arbitrary"), vmem_limit_bytes=VMEM_LIMIT_BYTES),
        name="hymba_layer",
    )(*operands)


def _mem_kv(mem, mem_norm_g, w_mem_kv, xk_norm_g):
    B = mem.shape[0]
    w_k, w_v = jnp.split(w_mem_kv, 2, axis=-1)
    w = jnp.concatenate([_permute_heads(w_k, 1), _permute_heads(w_v, 1)], axis=-1).astype(_BF16)
    head_mean = jnp.kron(jnp.eye(XATTN_HEADS, dtype=_F32),
                         jnp.full((HEAD_DIM, HEAD_DIM), 1.0 / HEAD_DIM, _F32)).astype(_BF16)
    operands = [mem, mem_norm_g.reshape(1, -1).astype(_F32), w,
                jnp.tile(xk_norm_g.astype(_F32), XATTN_HEADS).reshape(1, -1), head_mean]
    kv_shape = jax.ShapeDtypeStruct((B, MEM_LEN, XATTN_WIDTH), _BF16)
    kv_spec = pl.BlockSpec((1, MEM_LEN, XATTN_WIDTH), lambda b: (b, 0, 0))
    return pl.pallas_call(
        _mem_kv_kernel,
        out_shape=(kv_shape, kv_shape),
        grid=(B,),
        in_specs=[pl.BlockSpec((1, MEM_LEN, D_MODEL), lambda b: (b, 0, 0))]
        + [_const_spec(a.shape) for a in operands[1:]],
        out_specs=(kv_spec, kv_spec),
        compiler_params=pltpu.CompilerParams(dimension_semantics=("arbitrary",)),
        name="hymba_mem_kv",
    )(*operands)


def kernel(x, mem, norm_g, mem_norm_g, w_in, conv_w, conv_b, w_rg, b_rg, w_ig, b_ig, lru_lambda, q_norm_g,
           k_norm_g, sinks, w_mem_kv, xq_norm_g, xk_norm_g, out_norm_g, w_out):
    depth = w_in.shape[0]
    tile = 256
    assert x.shape[1] % tile == 0 and tile % BLOCK == 0
    h = x
    for l in range(depth):
        km, vm = _mem_kv(mem, mem_norm_g[l], w_mem_kv[l], xk_norm_g[l])
        params = dict(
            norm_g=norm_g[l], w_in=w_in[l], conv_w=conv_w[l], conv_b=conv_b[l], w_rg=w_rg[l], b_rg=b_rg[l],
            w_ig=w_ig[l], b_ig=b_ig[l], lru_lambda=lru_lambda[l], q_norm_g=q_norm_g[l], k_norm_g=k_norm_g[l],
            sinks=sinks[l], xq_norm_g=xq_norm_g[l], out_norm_g=out_norm_g[l], w_out=w_out[l])
        h = _layer(h, km, vm, params, tile=tile)
    return h
```

```python
import functools
import math

import jax
import jax.numpy as jnp
import numpy as np
from jax import lax
from jax.experimental import pallas as pl
from jax.experimental.pallas import tpu as pltpu

D_MODEL = 1024
MEM_LEN = 256
HEAD_DIM = 64
LRU_WIDTH = 512
LRU_BLOCKS = 8
LRU_BLOCK = LRU_WIDTH // LRU_BLOCKS
CONV_WIDTH = 4
LRU_C = 8.0
SWA_Q_HEADS = 4
SWA_KV_HEADS = 2
SWA_WIDTH = SWA_Q_HEADS * HEAD_DIM
SWA_KV_WIDTH = SWA_KV_HEADS * HEAD_DIM
WINDOW = 128
BLOCK = 128
XATTN_HEADS = 4
XATTN_WIDTH = XATTN_HEADS * HEAD_DIM
D_MIX = LRU_WIDTH + SWA_WIDTH + XATTN_WIDTH
ROPE_THETA = 500000.0
ROPE_DIM = HEAD_DIM // 4
EPS = 1e-6
NEG_INF = -1e30
LOG2_E = math.log2(math.e)

LANES = 128
SUBLANES = 8
MXU_DIM = 256
VMEM_LIMIT_BYTES = 56 * 1024 * 1024

TILE = 256
SEG_LEN = TILE // SUBLANES
SEG_PITCH = SEG_LEN + SUBLANES
LRU_SLABS = LRU_WIDTH // LANES
SQRT_CLAMP = float(np.finfo(np.float32).tiny)

_P_GATE_A = 0
_P_Q = _P_GATE_A + LRU_WIDTH
_P_K = _P_Q + SWA_WIDTH
_P_V = _P_K + SWA_KV_WIDTH
_P_GATE_B = _P_V + SWA_KV_WIDTH
_P_XQ = _P_GATE_B + SWA_WIDTH
_P_GATE_C = _P_XQ + XATTN_WIDTH
_P_END = _P_GATE_C + XATTN_WIDTH

_HEAD_SLOTS = ((0, 0), (1, 0), (0, 1), (1, 1))
_HEAD_PERM = (0, 2, 1, 3)

_BF16 = jnp.bfloat16
_F32 = jnp.float32


def _dot(a, b):
    return jnp.dot(a, b, preferred_element_type=_F32)


def _dot_nt(a, b):
    return lax.dot_general(a, b, (((1,), (1,)), ((), ())), preferred_element_type=_F32)


def _sigmoid(z):
    return 0.5 * jnp.tanh(0.5 * z) + 0.5


def _silu(z):
    hz = 0.5 * z
    return hz + hz * jnp.tanh(hz)


def _sqrt_nonneg(v):
    return v * lax.rsqrt(jnp.maximum(v, SQRT_CLAMP))


def _row_rms_scale(v):
    return lax.rsqrt(jnp.mean(v * v, axis=-1, keepdims=True) + EPS)


def _head_rms_scale(v, head_mean_ref):
    sq = (v * v).astype(_BF16)
    return lax.rsqrt(_dot(sq, head_mean_ref[...]) + EPS)


def _rope(v, cos, sin):
    half = ROPE_DIM // 2
    lane = lax.broadcasted_iota(jnp.int32, v.shape, 1) % HEAD_DIM
    partner = jnp.where(lane < half, pltpu.roll(v, LANES - half, axis=1), pltpu.roll(v, half, axis=1))
    return v * cos + partner * sin


def _mem_kv_kernel(mem_ref, g_ref, w_ref, kg_ref, head_mean_ref, km_ref, vm_ref):
    m = mem_ref[0]
    mn = (m * _row_rms_scale(m) * g_ref[...]).astype(_BF16)
    kv = _dot(mn, w_ref[...])
    k = kv[:, :XATTN_WIDTH]
    k = k * _head_rms_scale(k, head_mean_ref) * kg_ref[...]
    km_ref[0] = k.astype(_BF16)
    vm_ref[0] = kv[:, XATTN_WIDTH:].astype(_BF16)


def _layer_kernel(
    sinks_ref, x_ref, km_ref, vm_ref, cos_ref, sin_ref, norm_g_ref, w_in_ref, conv_w_ref, conv_b_ref,
    w_gate_ref, b_rg_ref, b_ig_ref, lam_ref, qg_ref, kg_ref, xqg_ref, head_mean_ref, out_g_ref, w_out_ref,
    o_ref,
    proj_scr, conv_scr, h_scr, carry_scr, tail_scr, k_scr, v_scr, y_scr,
    *, tile,
):
    s = pl.program_id(1)
    n_blocks = tile // BLOCK

    @pl.when(s == 0)
    def _reset_sequence_state():
        tail_scr[...] = jnp.zeros_like(tail_scr)
        carry_scr[...] = jnp.zeros_like(carry_scr)
        k_scr[tile:tile + BLOCK, :] = jnp.zeros((BLOCK, LANES), _BF16)
        v_scr[tile:tile + BLOCK, :] = jnp.zeros((BLOCK, LANES), _BF16)

    k_scr[0:BLOCK, :] = k_scr[tile:tile + BLOCK, :]
    v_scr[0:BLOCK, :] = v_scr[tile:tile + BLOCK, :]

    x = x_ref[0]
    xn = (x * _row_rms_scale(x) * norm_g_ref[...]).astype(_BF16)
    lru_x = _dot(xn, w_in_ref[:, 0:LRU_WIDTH])
    for seg in range(SUBLANES):
        for slab in range(LRU_SLABS):
            conv_scr[slab, seg * SEG_PITCH:seg * SEG_PITCH + SEG_LEN, :] = (
                lru_x[seg * SEG_LEN:(seg + 1) * SEG_LEN, slab * LANES:(slab + 1) * LANES])
    proj_scr[...] = _dot(xn, w_in_ref[:, LRU_WIDTH:])

    seg_id = lax.broadcasted_iota(jnp.int32, (SUBLANES, LRU_WIDTH), 0)

    def strided(ref, g):
        return jnp.concatenate(
            [ref[slab, pl.ds(g, SUBLANES, stride=SEG_PITCH), :] for slab in range(LRU_SLABS)], axis=-1)

    def shift_segments(cur, prev_tile):
        return jnp.where(seg_id == 0, pltpu.roll(prev_tile, 1, axis=0), pltpu.roll(cur, 1, axis=0))

    u = [strided(conv_scr, g) for g in range(SEG_LEN)]
    wrapped = {}
    for k in range(1, CONV_WIDTH):
        wrapped[SEG_LEN - k] = shift_segments(u[SEG_LEN - k], tail_scr[(k - 1) * SUBLANES:k * SUBLANES, :])
    for k in range(1, CONV_WIDTH):
        tail_scr[(k - 1) * SUBLANES:k * SUBLANES, :] = u[SEG_LEN - k]
    xc_parts = []
    for g in range(SEG_LEN):
        xc = conv_b_ref[...] + conv_w_ref[CONV_WIDTH - 1:CONV_WIDTH, :] * u[g]
        for k in range(1, CONV_WIDTH):
            past = u[g - k] if g >= k else wrapped[SEG_LEN + g - k]
            xc = xc + conv_w_ref[CONV_WIDTH - 1 - k:CONV_WIDTH - k, :] * past
        xc_parts.append(xc)
    xc = jnp.concatenate(xc_parts, axis=0)

    lam = -lam_ref[...]
    softplus = jnp.maximum(lam, 0.0) + jnp.log1p(jnp.exp(-jnp.abs(lam)))
    neg_c_softplus = -LRU_C * softplus
    a_parts, b_parts = [], []
    for j in range(LRU_WIDTH // MXU_DIM):
        cols = slice(j * MXU_DIM, (j + 1) * MXU_DIM)
        pre = _dot(xc[:, cols].astype(_BF16), w_gate_ref[j])
        rg = _sigmoid(pre[:, :MXU_DIM] + b_rg_ref[:, cols])
        ig = _sigmoid(pre[:, MXU_DIM:] + b_ig_ref[:, cols])
        log_a = rg * neg_c_softplus[:, cols]
        a = jnp.exp(log_a)
        a_parts.append(a)
        b_parts.append(_sqrt_nonneg(jnp.tanh(-log_a) * (1.0 + a * a)) * (ig * xc[:, cols]))
    a_all = jnp.concatenate(a_parts, axis=-1)
    b_all = jnp.concatenate(b_parts, axis=-1)

    local, decay = [], []
    for g in range(SEG_LEN):
        a_g = a_all[g * SUBLANES:(g + 1) * SUBLANES, :]
        b_g = b_all[g * SUBLANES:(g + 1) * SUBLANES, :]
        local.append(b_g if g == 0 else a_g * local[-1] + b_g)
        decay.append(a_g if g == 0 else a_g * decay[-1])
    seg_a, seg_b = decay[-1], local[-1]
    d = 1
    while d < SUBLANES:
        keep = seg_id >= d
        a_prev = jnp.where(keep, pltpu.roll(seg_a, d, axis=0), 1.0)
        b_prev = jnp.where(keep, pltpu.roll(seg_b, d, axis=0), 0.0)
        seg_b = seg_b + seg_a * b_prev
        seg_a = seg_a * a_prev
        d *= 2
    carry = carry_scr[...]
    seg_end = seg_a * carry + seg_b
    seg_in = jnp.where(seg_id == 0, carry, pltpu.roll(seg_end, 1, axis=0))
    carry_scr[...] = seg_end[SUBLANES - 1:SUBLANES, :]
    for g in range(SEG_LEN):
        h_g = local[g] + decay[g] * seg_in
        for slab in range(LRU_SLABS):
            h_scr[slab, pl.ds(g, SUBLANES, stride=SEG_PITCH), :] = h_g[:, slab * LANES:(slab + 1) * LANES]

    h = jnp.concatenate([
        jnp.concatenate([h_scr[slab, seg * SEG_PITCH:seg * SEG_PITCH + SEG_LEN, :] for slab in range(LRU_SLABS)],
                        axis=-1)
        for seg in range(SUBLANES)], axis=0)
    gate_a = proj_scr[:, _P_GATE_A:_P_GATE_A + LRU_WIDTH]
    y_a = h * _row_rms_scale(h) * out_g_ref[:, 0:LRU_WIDTH] * _silu(gate_a)
    y_scr[:, 0:LRU_WIDTH] = y_a.astype(_BF16)

    pos = pl.ds(pl.multiple_of(s * tile, tile), tile)
    cos = cos_ref[pos, :]
    sin = sin_ref[pos, :]
    score_scale = LOG2_E / math.sqrt(HEAD_DIM)

    q_groups = []
    for g in range(SWA_WIDTH // LANES):
        cols = slice(_P_Q + g * LANES, _P_Q + (g + 1) * LANES)
        qraw = proj_scr[:, cols]
        qn = qraw * _head_rms_scale(qraw, head_mean_ref) * qg_ref[...]
        q_groups.append(_rope(qn, cos, sin) * score_scale)
    kraw = proj_scr[:, _P_K:_P_K + SWA_KV_WIDTH]
    kn = kraw * _head_rms_scale(kraw, head_mean_ref) * kg_ref[...]
    k_scr[BLOCK:BLOCK + tile, :] = _rope(kn, cos, sin).astype(_BF16)
    v_scr[BLOCK:BLOCK + tile, :] = proj_scr[:, _P_V:_P_V + SWA_KV_WIDTH].astype(_BF16)

    xq_groups = []
    for g in range(XATTN_WIDTH // LANES):
        cols = slice(_P_XQ + g * LANES, _P_XQ + (g + 1) * LANES)
        xraw = proj_scr[:, cols]
        xq_groups.append(xraw * _head_rms_scale(xraw, head_mean_ref) * xqg_ref[...] * score_scale)

    lane = lax.broadcasted_iota(jnp.int32, (BLOCK, LANES), 1)
    low_half = lane < HEAD_DIM
    half_masks = (low_half, lane >= HEAD_DIM)
    qi = lax.broadcasted_iota(jnp.int32, (BLOCK, BLOCK), 0)
    kj = lax.broadcasted_iota(jnp.int32, (BLOCK, BLOCK), 1)
    own_block = kj <= qi

    for blk in range(n_blocks):
        rows = slice(blk * BLOCK, (blk + 1) * BLOCK)
        kk = k_scr[blk * BLOCK:(blk + 2) * BLOCK, :]
        vv = v_scr[blk * BLOCK:(blk + 2) * BLOCK, :]
        o_swa = [None, None]
        for head, (g, hi) in enumerate(_HEAD_SLOTS):
            qm = jnp.where(half_masks[hi], q_groups[g][rows, :], 0.0).astype(_BF16)
            sc_both = _dot_nt(qm, kk)
            sc_prev = sc_both[:, :BLOCK]
            if blk == 0:
                sc_prev = jnp.where(s > 0, sc_prev, NEG_INF)
            sc = jnp.where(own_block, sc_both[:, BLOCK:], sc_prev)
            sink = sinks_ref[head] * LOG2_E
            m = jnp.maximum(jnp.max(sc, axis=-1, keepdims=True), sink)
            p = jnp.exp2(sc - m)
            den = jnp.sum(p, axis=-1, keepdims=True) + jnp.exp2(sink - m)
            p_both = jnp.concatenate([jnp.where(own_block, 0.0, p), jnp.where(own_block, p, 0.0)], axis=-1)
            o = _dot(p_both.astype(_BF16), vv) * (1.0 / den)
            o_swa[g] = o if o_swa[g] is None else jnp.where(low_half, o_swa[g], o)
        y_b = jnp.concatenate(o_swa, axis=-1)
        gate_b = proj_scr[rows, _P_GATE_B:_P_GATE_B + SWA_WIDTH]
        y_b = y_b * _row_rms_scale(y_b) * out_g_ref[:, LRU_WIDTH:LRU_WIDTH + SWA_WIDTH] * _silu(gate_b)
        y_scr[rows, LRU_WIDTH:LRU_WIDTH + SWA_WIDTH] = y_b.astype(_BF16)

        o_mem = [None, None]
        for head, (g, hi) in enumerate(_HEAD_SLOTS):
            qm = jnp.where(half_masks[hi], xq_groups[g][rows, :], 0.0).astype(_BF16)
            sc = _dot_nt(qm, km_ref[0, :, g * LANES:(g + 1) * LANES])
            m = jnp.max(sc, axis=-1, keepdims=True)
            p = jnp.exp2(sc - m)
            den = jnp.sum(p, axis=-1, keepdims=True)
            o = _dot(p.astype(_BF16), vm_ref[0, :, g * LANES:(g + 1) * LANES]) * (1.0 / den)
            o_mem[g] = o if o_mem[g] is None else jnp.where(low_half, o_mem[g], o)
        y_c = jnp.concatenate(o_mem, axis=-1)
        gate_c = proj_scr[rows, _P_GATE_C:_P_GATE_C + XATTN_WIDTH]
        y_c = y_c * _row_rms_scale(y_c) * out_g_ref[:, LRU_WIDTH + SWA_WIDTH:] * _silu(gate_c)
        y_scr[rows, LRU_WIDTH + SWA_WIDTH:] = y_c.astype(_BF16)

    o_ref[0] = x + _dot(y_scr[...], w_out_ref[...])


def _permute_heads(w, axis):
    shape = w.shape
    split = shape[:axis] + (4, HEAD_DIM) + shape[axis + 1:]
    return jnp.take(w.reshape(split), jnp.array(_HEAD_PERM), axis=axis).reshape(shape)


def _rope_tables(seq):
    pos = jnp.arange(seq, dtype=_F32)
    inv_freq = ROPE_THETA ** (-(jnp.arange(0, ROPE_DIM, 2, dtype=_F32) / ROPE_DIM))
    ang = pos[:, None] * inv_freq[None, :]
    cos, sin = jnp.cos(ang), jnp.sin(ang)
    ones = jnp.ones((seq, HEAD_DIM - ROPE_DIM), _F32)
    cos_head = jnp.concatenate([cos, cos, ones], axis=-1)
    sin_head = jnp.concatenate([-sin, sin, 0.0 * ones], axis=-1)
    reps = LANES // HEAD_DIM
    return jnp.tile(cos_head, (1, reps)), jnp.tile(sin_head, (1, reps))


def _const_spec(shape):
    return pl.BlockSpec(shape, lambda *_: (0,) * len(shape))


def _layer(x, km, vm, p, *, tile):
    B, S, _ = x.shape
    n_tiles = S // tile
    row = lambda v: v.reshape(1, -1).astype(_F32)

    lru_x, gate_a, sq, sk, sv, gate_b, xq, gate_c = jnp.split(
        p["w_in"], list(np.cumsum((LRU_WIDTH, LRU_WIDTH, SWA_WIDTH, SWA_KV_WIDTH, SWA_KV_WIDTH, SWA_WIDTH,
                                    XATTN_WIDTH))), axis=-1)
    w_in = jnp.concatenate(
        [lru_x, gate_a, _permute_heads(sq, 1), sk, sv, _permute_heads(gate_b, 1), _permute_heads(xq, 1),
         _permute_heads(gate_c, 1)], axis=-1).astype(_BF16)

    def block_diag(w):
        eye = jnp.eye(LRU_BLOCKS, dtype=w.dtype)
        return jnp.einsum("nde,nm->ndme", w, eye).reshape(LRU_WIDTH, LRU_WIDTH)

    w_r, w_i = block_diag(p["w_rg"]), block_diag(p["w_ig"])
    w_gate = jnp.stack([
        jnp.concatenate([w_r[c:c + MXU_DIM, c:c + MXU_DIM], w_i[c:c + MXU_DIM, c:c + MXU_DIM]], axis=-1)
        for c in range(0, LRU_WIDTH, MXU_DIM)]).astype(_BF16)

    g_a, g_b, g_c = jnp.split(p["out_norm_g"], [LRU_WIDTH, LRU_WIDTH + SWA_WIDTH])
    out_g = jnp.concatenate([g_a, _permute_heads(g_b, 0), _permute_heads(g_c, 0)])
    w_a, w_b, w_c = jnp.split(p["w_out"], [LRU_WIDTH, LRU_WIDTH + SWA_WIDTH], axis=0)
    w_out = jnp.concatenate([w_a, _permute_heads(w_b, 0), _permute_heads(w_c, 0)], axis=0).astype(_BF16)

    head_mean = jnp.kron(jnp.eye(LANES // HEAD_DIM, dtype=_F32),
                         jnp.full((HEAD_DIM, HEAD_DIM), 1.0 / HEAD_DIM, _F32)).astype(_BF16)
    cos, sin = _rope_tables(S)
    tile_heads = lambda g: jnp.tile(g.astype(_F32), LANES // HEAD_DIM).reshape(1, LANES)

    operands = [
        p["sinks"].astype(_F32),
        x, km, vm, cos, sin, row(p["norm_g"]), w_in, p["conv_w"].astype(_F32), row(p["conv_b"]),
        w_gate, row(p["b_rg"]), row(p["b_ig"]), row(p["lru_lambda"]),
        tile_heads(p["q_norm_g"]), tile_heads(p["k_norm_g"]), tile_heads(p["xq_norm_g"]),
        head_mean, row(out_g), w_out,
    ]
    in_specs = [
        pl.BlockSpec(memory_space=pltpu.SMEM),
        pl.BlockSpec((1, tile, D_MODEL), lambda b, s: (b, s, 0)),
        pl.BlockSpec((1, MEM_LEN, XATTN_WIDTH), lambda b, s: (b, 0, 0)),
        pl.BlockSpec((1, MEM_LEN, XATTN_WIDTH), lambda b, s: (b, 0, 0)),
    ] + [_const_spec(a.shape) for a in operands[4:]]

    return pl.pallas_call(
        functools.partial(_layer_kernel, tile=tile),
        out_shape=jax.ShapeDtypeStruct(x.shape, x.dtype),
        grid=(B, n_tiles),
        in_specs=in_specs,
        out_specs=pl.BlockSpec((1, tile, D_MODEL), lambda b, s: (b, s, 0)),
        scratch_shapes=[
            pltpu.VMEM((tile, _P_END), _F32),
            pltpu.VMEM((LRU_SLABS, SUBLANES * SEG_PITCH, LANES), _F32),
            pltpu.VMEM((LRU_SLABS, SUBLANES * SEG_PITCH, LANES), _F32),
            pltpu.VMEM((1, LRU_WIDTH), _F32),
            pltpu.VMEM(((CONV_WIDTH - 1) * SUBLANES, LRU_WIDTH), _F32),
            pltpu.VMEM((tile + BLOCK, LANES), _BF16),
            pltpu.VMEM((tile + BLOCK, LANES), _BF16),
            pltpu.VMEM((tile, D_MIX), _BF16),
        ],
        compiler_params=pltpu.CompilerParams(
            dimension_semantics=("arbitrary", "arbitrary"), vmem_limit_bytes=VMEM_LIMIT_BYTES),
        name="hymba_layer",
    )(*operands)


def _mem_kv(mem, mem_norm_g, w_mem_kv, xk_norm_g):
    B = mem.shape[0]
    w_k, w_v = jnp.split(w_mem_kv, 2, axis=-1)
    w = jnp.concatenate([_permute_heads(w_k, 1), _permute_heads(w_v, 1)], axis=-1).astype(_BF16)
    head_mean = jnp.kron(jnp.eye(XATTN_HEADS, dtype=_F32),
                         jnp.full((HEAD_DIM, HEAD_DIM), 1.0 / HEAD_DIM, _F32)).astype(_BF16)
    operands = [mem, mem_norm_g.reshape(1, -1).astype(_F32), w,
                jnp.tile(xk_norm_g.astype(_F32), XATTN_HEADS).reshape(1, -1), head_mean]
    kv_shape = jax.ShapeDtypeStruct((B, MEM_LEN, XATTN_WIDTH), _BF16)
    kv_spec = pl.BlockSpec((1, MEM_LEN, XATTN_WIDTH), lambda b: (b, 0, 0))
    return pl.pallas_call(
        _mem_kv_kernel,
        out_shape=(kv_shape, kv_shape),
        grid=(B,),
        in_specs=[pl.BlockSpec((1, MEM_LEN, D_MODEL), lambda b: (b, 0, 0))]
        + [_const_spec(a.shape) for a in operands[1:]],
        out_specs=(kv_spec, kv_spec),
        compiler_params=pltpu.CompilerParams(dimension_semantics=("arbitrary",)),
        name="hymba_mem_kv",
    )(*operands)


def kernel(x, mem, norm_g, mem_norm_g, w_in, conv_w, conv_b, w_rg, b_rg, w_ig, b_ig, lru_lambda, q_norm_g,
           k_norm_g, sinks, w_mem_kv, xq_norm_g, xk_norm_g, out_norm_g, w_out):
    depth = w_in.shape[0]
    tile = TILE
    assert x.shape[1] % tile == 0 and tile % BLOCK == 0 and WINDOW == BLOCK
    h = x
    for l in range(depth):
        km, vm = _mem_kv(mem, mem_norm_g[l], w_mem_kv[l], xk_norm_g[l])
        params = dict(
            norm_g=norm_g[l], w_in=w_in[l], conv_w=conv_w[l], conv_b=conv_b[l], w_rg=w_rg[l], b_rg=b_rg[l],
            w_ig=w_ig[l], b_ig=b_ig[l], lru_lambda=lru_lambda[l], q_norm_g=q_norm_g[l], k_norm_g=k_norm_g[l],
            sinks=sinks[l], xq_norm_g=xq_norm_g[l], out_norm_g=out_norm_g[l], w_out=w_out[l])
        h = _layer(h, km, vm, params, tile=tile)
    return h
```

```python
import functools
import math

import jax
import jax.numpy as jnp
import numpy as np
from jax import lax
from jax.experimental import pallas as pl
from jax.experimental.pallas import tpu as pltpu

D_MODEL = 1024
MEM_LEN = 256
HEAD_DIM = 64
LRU_WIDTH = 512
LRU_BLOCKS = 8
LRU_BLOCK = LRU_WIDTH // LRU_BLOCKS
CONV_WIDTH = 4
LRU_C = 8.0
SWA_Q_HEADS = 4
SWA_KV_HEADS = 2
SWA_WIDTH = SWA_Q_HEADS * HEAD_DIM
SWA_KV_WIDTH = SWA_KV_HEADS * HEAD_DIM
WINDOW = 128
BLOCK = 128
XATTN_HEADS = 4
XATTN_WIDTH = XATTN_HEADS * HEAD_DIM
D_MIX = LRU_WIDTH + SWA_WIDTH + XATTN_WIDTH
ROPE_THETA = 500000.0
ROPE_DIM = HEAD_DIM // 4
EPS = 1e-6
NEG_INF = -1e30
LOG2_E = math.log2(math.e)

LANES = 128
SUBLANES = 8
MXU_DIM = 256
VMEM_LIMIT_BYTES = 56 * 1024 * 1024

TILE = 512
SEG_LEN = TILE // SUBLANES
SEG_PITCH = SEG_LEN + SUBLANES
LRU_SLABS = LRU_WIDTH // LANES
SQRT_CLAMP = float(np.finfo(np.float32).tiny)

_P_GATE_A = 0
_P_Q = _P_GATE_A + LRU_WIDTH
_P_K = _P_Q + SWA_WIDTH
_P_V = _P_K + SWA_KV_WIDTH
_P_GATE_B = _P_V + SWA_KV_WIDTH
_P_XQ = _P_GATE_B + SWA_WIDTH
_P_GATE_C = _P_XQ + XATTN_WIDTH
_P_END = _P_GATE_C + XATTN_WIDTH

_HEAD_SLOTS = ((0, 0), (1, 0), (0, 1), (1, 1))
_HEAD_PERM = (0, 2, 1, 3)

_BF16 = jnp.bfloat16
_F32 = jnp.float32


def _dot(a, b):
    return jnp.dot(a, b, preferred_element_type=_F32)


def _dot_nt(a, b):
    return lax.dot_general(a, b, (((1,), (1,)), ((), ())), preferred_element_type=_F32)


def _sigmoid(z):
    return 0.5 * jnp.tanh(0.5 * z) + 0.5


def _silu(z):
    hz = 0.5 * z
    return hz + hz * jnp.tanh(hz)


def _sqrt_nonneg(v):
    return v * lax.rsqrt(jnp.maximum(v, SQRT_CLAMP))


def _row_rms_scale(v):
    return lax.rsqrt(jnp.mean(v * v, axis=-1, keepdims=True) + EPS)


def _head_rms_scale(v, head_mean_ref):
    sq = (v * v).astype(_BF16)
    return lax.rsqrt(_dot(sq, head_mean_ref[...]) + EPS)


def _rope(v, cos, sin):
    half = ROPE_DIM // 2
    lane = lax.broadcasted_iota(jnp.int32, v.shape, 1) % HEAD_DIM
    partner = jnp.where(lane < half, pltpu.roll(v, LANES - half, axis=1), pltpu.roll(v, half, axis=1))
    return v * cos + partner * sin


def _mem_kv_kernel(mem_ref, g_ref, w_ref, kg_ref, head_mean_ref, km_ref, vm_ref):
    m = mem_ref[0]
    mn = (m * _row_rms_scale(m) * g_ref[...]).astype(_BF16)
    kv = _dot(mn, w_ref[...])
    k = kv[:, :XATTN_WIDTH]
    k = k * _head_rms_scale(k, head_mean_ref) * kg_ref[...]
    km_ref[0] = k.astype(_BF16)
    vm_ref[0] = kv[:, XATTN_WIDTH:].astype(_BF16)


def _layer_kernel(
    sinks_ref, x_ref, km_ref, vm_ref, cos_ref, sin_ref, norm_g_ref, w_in_ref, conv_w_ref, conv_b_ref,
    w_gate_ref, b_rg_ref, b_ig_ref, lam_ref, qg_ref, kg_ref, xqg_ref, head_mean_ref, out_g_ref, w_out_ref,
    o_ref,
    proj_scr, conv_scr, h_scr, carry_scr, tail_scr, k_scr, v_scr, y_scr,
    *, tile,
):
    s = pl.program_id(1)
    n_blocks = tile // BLOCK

    @pl.when(s == 0)
    def _reset_sequence_state():
        tail_scr[...] = jnp.zeros_like(tail_scr)
        carry_scr[...] = jnp.zeros_like(carry_scr)
        k_scr[tile:tile + BLOCK, :] = jnp.zeros((BLOCK, LANES), _BF16)
        v_scr[tile:tile + BLOCK, :] = jnp.zeros((BLOCK, LANES), _BF16)

    k_scr[0:BLOCK, :] = k_scr[tile:tile + BLOCK, :]
    v_scr[0:BLOCK, :] = v_scr[tile:tile + BLOCK, :]

    x = x_ref[0]
    xn = (x * _row_rms_scale(x) * norm_g_ref[...]).astype(_BF16)
    lru_x = _dot(xn, w_in_ref[:, 0:LRU_WIDTH])
    for seg in range(SUBLANES):
        for slab in range(LRU_SLABS):
            conv_scr[slab, seg * SEG_PITCH:seg * SEG_PITCH + SEG_LEN, :] = (
                lru_x[seg * SEG_LEN:(seg + 1) * SEG_LEN, slab * LANES:(slab + 1) * LANES])
    proj_scr[...] = _dot(xn, w_in_ref[:, LRU_WIDTH:])

    seg_id = lax.broadcasted_iota(jnp.int32, (SUBLANES, LRU_WIDTH), 0)

    def strided(ref, g):
        return jnp.concatenate(
            [ref[slab, pl.ds(g, SUBLANES, stride=SEG_PITCH), :] for slab in range(LRU_SLABS)], axis=-1)

    def shift_segments(cur, prev_tile):
        return jnp.where(seg_id == 0, pltpu.roll(prev_tile, 1, axis=0), pltpu.roll(cur, 1, axis=0))

    u = [strided(conv_scr, g) for g in range(SEG_LEN)]
    wrapped = {}
    for k in range(1, CONV_WIDTH):
        wrapped[SEG_LEN - k] = shift_segments(u[SEG_LEN - k], tail_scr[(k - 1) * SUBLANES:k * SUBLANES, :])
    for k in range(1, CONV_WIDTH):
        tail_scr[(k - 1) * SUBLANES:k * SUBLANES, :] = u[SEG_LEN - k]
    xc_parts = []
    for g in range(SEG_LEN):
        xc = conv_b_ref[...] + conv_w_ref[CONV_WIDTH - 1:CONV_WIDTH, :] * u[g]
        for k in range(1, CONV_WIDTH):
            past = u[g - k] if g >= k else wrapped[SEG_LEN + g - k]
            xc = xc + conv_w_ref[CONV_WIDTH - 1 - k:CONV_WIDTH - k, :] * past
        xc_parts.append(xc)
    xc = jnp.concatenate(xc_parts, axis=0)

    lam = -lam_ref[...]
    softplus = jnp.maximum(lam, 0.0) + jnp.log1p(jnp.exp(-jnp.abs(lam)))
    neg_c_softplus = -LRU_C * softplus
    a_parts, b_parts = [], []
    for j in range(LRU_WIDTH // MXU_DIM):
        cols = slice(j * MXU_DIM, (j + 1) * MXU_DIM)
        pre = _dot(xc[:, cols].astype(_BF16), w_gate_ref[j])
        rg = _sigmoid(pre[:, :MXU_DIM] + b_rg_ref[:, cols])
        ig = _sigmoid(pre[:, MXU_DIM:] + b_ig_ref[:, cols])
        log_a = rg * neg_c_softplus[:, cols]
        a = jnp.exp(log_a)
        a_parts.append(a)
        b_parts.append(_sqrt_nonneg(jnp.tanh(-log_a) * (1.0 + a * a)) * (ig * xc[:, cols]))
    a_all = jnp.concatenate(a_parts, axis=-1)
    b_all = jnp.concatenate(b_parts, axis=-1)

    local, decay = [], []
    for g in range(SEG_LEN):
        a_g = a_all[g * SUBLANES:(g + 1) * SUBLANES, :]
        b_g = b_all[g * SUBLANES:(g + 1) * SUBLANES, :]
        local.append(b_g if g == 0 else a_g * local[-1] + b_g)
        decay.append(a_g if g == 0 else a_g * decay[-1])
    seg_a, seg_b = decay[-1], local[-1]
    d = 1
    while d < SUBLANES:
        keep = seg_id >= d
        a_prev = jnp.where(keep, pltpu.roll(seg_a, d, axis=0), 1.0)
        b_prev = jnp.where(keep, pltpu.roll(seg_b, d, axis=0), 0.0)
        seg_b = seg_b + seg_a * b_prev
        seg_a = seg_a * a_prev
        d *= 2
    carry = carry_scr[...]
    seg_end = seg_a * carry + seg_b
    seg_in = jnp.where(seg_id == 0, carry, pltpu.roll(seg_end, 1, axis=0))
    carry_scr[...] = seg_end[SUBLANES - 1:SUBLANES, :]
    for g in range(SEG_LEN):
        h_g = local[g] + decay[g] * seg_in
        for slab in range(LRU_SLABS):
            h_scr[slab, pl.ds(g, SUBLANES, stride=SEG_PITCH), :] = h_g[:, slab * LANES:(slab + 1) * LANES]

    h = jnp.concatenate([
        jnp.concatenate([h_scr[slab, seg * SEG_PITCH:seg * SEG_PITCH + SEG_LEN, :] for slab in range(LRU_SLABS)],
                        axis=-1)
        for seg in range(SUBLANES)], axis=0)
    gate_a = proj_scr[:, _P_GATE_A:_P_GATE_A + LRU_WIDTH]
    y_a = h * _row_rms_scale(h) * out_g_ref[:, 0:LRU_WIDTH] * _silu(gate_a)
    y_scr[:, 0:LRU_WIDTH] = y_a.astype(_BF16)

    pos = pl.ds(pl.multiple_of(s * tile, tile), tile)
    cos = cos_ref[pos, :]
    sin = sin_ref[pos, :]
    score_scale = LOG2_E / math.sqrt(HEAD_DIM)

    q_groups = []
    for g in range(SWA_WIDTH // LANES):
        cols = slice(_P_Q + g * LANES, _P_Q + (g + 1) * LANES)
        qraw = proj_scr[:, cols]
        qn = qraw * _head_rms_scale(qraw, head_mean_ref) * qg_ref[...]
        q_groups.append(_rope(qn, cos, sin) * score_scale)
    kraw = proj_scr[:, _P_K:_P_K + SWA_KV_WIDTH]
    kn = kraw * _head_rms_scale(kraw, head_mean_ref) * kg_ref[...]
    k_scr[BLOCK:BLOCK + tile, :] = _rope(kn, cos, sin).astype(_BF16)
    v_scr[BLOCK:BLOCK + tile, :] = proj_scr[:, _P_V:_P_V + SWA_KV_WIDTH].astype(_BF16)

    xq_groups = []
    for g in range(XATTN_WIDTH // LANES):
        cols = slice(_P_XQ + g * LANES, _P_XQ + (g + 1) * LANES)
        xraw = proj_scr[:, cols]
        xq_groups.append(xraw * _head_rms_scale(xraw, head_mean_ref) * xqg_ref[...] * score_scale)

    lane = lax.broadcasted_iota(jnp.int32, (BLOCK, LANES), 1)
    low_half = lane < HEAD_DIM
    half_masks = (low_half, lane >= HEAD_DIM)
    qi = lax.broadcasted_iota(jnp.int32, (BLOCK, BLOCK), 0)
    kj = lax.broadcasted_iota(jnp.int32, (BLOCK, BLOCK), 1)
    own_block = kj <= qi

    for blk in range(n_blocks):
        rows = slice(blk * BLOCK, (blk + 1) * BLOCK)
        kk = k_scr[blk * BLOCK:(blk + 2) * BLOCK, :]
        vv = v_scr[blk * BLOCK:(blk + 2) * BLOCK, :]
        o_swa = [None, None]
        for head, (g, hi) in enumerate(_HEAD_SLOTS):
            qm = jnp.where(half_masks[hi], q_groups[g][rows, :], 0.0).astype(_BF16)
            sc_both = _dot_nt(qm, kk)
            sc_prev = sc_both[:, :BLOCK]
            if blk == 0:
                sc_prev = jnp.where(s > 0, sc_prev, NEG_INF)
            sc = jnp.where(own_block, sc_both[:, BLOCK:], sc_prev)
            sink = sinks_ref[head] * LOG2_E
            m = jnp.maximum(jnp.max(sc, axis=-1, keepdims=True), sink)
            p = jnp.exp2(sc - m)
            den = jnp.sum(p, axis=-1, keepdims=True) + jnp.exp2(sink - m)
            p_both = jnp.concatenate([jnp.where(own_block, 0.0, p), jnp.where(own_block, p, 0.0)], axis=-1)
            o = _dot(p_both.astype(_BF16), vv) * (1.0 / den)
            o_swa[g] = o if o_swa[g] is None else jnp.where(low_half, o_swa[g], o)
        y_b = jnp.concatenate(o_swa, axis=-1)
        gate_b = proj_scr[rows, _P_GATE_B:_P_GATE_B + SWA_WIDTH]
        y_b = y_b * _row_rms_scale(y_b) * out_g_ref[:, LRU_WIDTH:LRU_WIDTH + SWA_WIDTH] * _silu(gate_b)
        y_scr[rows, LRU_WIDTH:LRU_WIDTH + SWA_WIDTH] = y_b.astype(_BF16)

        o_mem = [None, None]
        for head, (g, hi) in enumerate(_HEAD_SLOTS):
            qm = jnp.where(half_masks[hi], xq_groups[g][rows, :], 0.0).astype(_BF16)
            sc = _dot_nt(qm, km_ref[0, :, g * LANES:(g + 1) * LANES])
            m = jnp.max(sc, axis=-1, keepdims=True)
            p = jnp.exp2(sc - m)
            den = jnp.sum(p, axis=-1, keepdims=True)
            o = _dot(p.astype(_BF16), vm_ref[0, :, g * LANES:(g + 1) * LANES]) * (1.0 / den)
            o_mem[g] = o if o_mem[g] is None else jnp.where(low_half, o_mem[g], o)
        y_c = jnp.concatenate(o_mem, axis=-1)
        gate_c = proj_scr[rows, _P_GATE_C:_P_GATE_C + XATTN_WIDTH]
        y_c = y_c * _row_rms_scale(y_c) * out_g_ref[:, LRU_WIDTH + SWA_WIDTH:] * _silu(gate_c)
        y_scr[rows, LRU_WIDTH + SWA_WIDTH:] = y_c.astype(_BF16)

    o_ref[0] = x + _dot(y_scr[...], w_out_ref[...])


def _permute_heads(w, axis):
    shape = w.shape
    split = shape[:axis] + (4, HEAD_DIM) + shape[axis + 1:]
    return jnp.take(w.reshape(split), jnp.array(_HEAD_PERM), axis=axis).reshape(shape)


def _rope_tables(seq):
    pos = jnp.arange(seq, dtype=_F32)
    inv_freq = ROPE_THETA ** (-(jnp.arange(0, ROPE_DIM, 2, dtype=_F32) / ROPE_DIM))
    ang = pos[:, None] * inv_freq[None, :]
    cos, sin = jnp.cos(ang), jnp.sin(ang)
    ones = jnp.ones((seq, HEAD_DIM - ROPE_DIM), _F32)
    cos_head = jnp.concatenate([cos, cos, ones], axis=-1)
    sin_head = jnp.concatenate([-sin, sin, 0.0 * ones], axis=-1)
    reps = LANES // HEAD_DIM
    return jnp.tile(cos_head, (1, reps)), jnp.tile(sin_head, (1, reps))


def _const_spec(shape):
    return pl.BlockSpec(shape, lambda *_: (0,) * len(shape))


def _layer(x, km, vm, p, *, tile):
    B, S, _ = x.shape
    n_tiles = S // tile
    row = lambda v: v.reshape(1, -1).astype(_F32)

    lru_x, gate_a, sq, sk, sv, gate_b, xq, gate_c = jnp.split(
        p["w_in"], list(np.cumsum((LRU_WIDTH, LRU_WIDTH, SWA_WIDTH, SWA_KV_WIDTH, SWA_KV_WIDTH, SWA_WIDTH,
                                    XATTN_WIDTH))), axis=-1)
    w_in = jnp.concatenate(
        [lru_x, gate_a, _permute_heads(sq, 1), sk, sv, _permute_heads(gate_b, 1), _permute_heads(xq, 1),
         _permute_heads(gate_c, 1)], axis=-1).astype(_BF16)

    def block_diag(w):
        eye = jnp.eye(LRU_BLOCKS, dtype=w.dtype)
        return jnp.einsum("nde,nm->ndme", w, eye).reshape(LRU_WIDTH, LRU_WIDTH)

    w_r, w_i = block_diag(p["w_rg"]), block_diag(p["w_ig"])
    w_gate = jnp.stack([
        jnp.concatenate([w_r[c:c + MXU_DIM, c:c + MXU_DIM], w_i[c:c + MXU_DIM, c:c + MXU_DIM]], axis=-1)
        for c in range(0, LRU_WIDTH, MXU_DIM)]).astype(_BF16)

    g_a, g_b, g_c = jnp.split(p["out_norm_g"], [LRU_WIDTH, LRU_WIDTH + SWA_WIDTH])
    out_g = jnp.concatenate([g_a, _permute_heads(g_b, 0), _permute_heads(g_c, 0)])
    w_a, w_b, w_c = jnp.split(p["w_out"], [LRU_WIDTH, LRU_WIDTH + SWA_WIDTH], axis=0)
    w_out = jnp.concatenate([w_a, _permute_heads(w_b, 0), _permute_heads(w_c, 0)], axis=0).astype(_BF16)

    head_mean = jnp.kron(jnp.eye(LANES // HEAD_DIM, dtype=_F32),
                         jnp.full((HEAD_DIM, HEAD_DIM), 1.0 / HEAD_DIM, _F32)).astype(_BF16)
    cos, sin = _rope_tables(S)
    tile_heads = lambda g: jnp.tile(g.astype(_F32), LANES // HEAD_DIM).reshape(1, LANES)

    operands = [
        p["sinks"].astype(_F32),
        x, km, vm, cos, sin, row(p["norm_g"]), w_in, p["conv_w"].astype(_F32), row(p["conv_b"]),
        w_gate, row(p["b_rg"]), row(p["b_ig"]), row(p["lru_lambda"]),
        tile_heads(p["q_norm_g"]), tile_heads(p["k_norm_g"]), tile_heads(p["xq_norm_g"]),
        head_mean, row(out_g), w_out,
    ]
    in_specs = [
        pl.BlockSpec(memory_space=pltpu.SMEM),
        pl.BlockSpec((1, tile, D_MODEL), lambda b, s: (b, s, 0)),
        pl.BlockSpec((1, MEM_LEN, XATTN_WIDTH), lambda b, s: (b, 0, 0)),
        pl.BlockSpec((1, MEM_LEN, XATTN_WIDTH), lambda b, s: (b, 0, 0)),
    ] + [_const_spec(a.shape) for a in operands[4:]]

    return pl.pallas_call(
        functools.partial(_layer_kernel, tile=tile),
        out_shape=jax.ShapeDtypeStruct(x.shape, x.dtype),
        grid=(B, n_tiles),
        in_specs=in_specs,
        out_specs=pl.BlockSpec((1, tile, D_MODEL), lambda b, s: (b, s, 0)),
        scratch_shapes=[
            pltpu.VMEM((tile, _P_END), _F32),
            pltpu.VMEM((LRU_SLABS, SUBLANES * SEG_PITCH, LANES), _F32),
            pltpu.VMEM((LRU_SLABS, SUBLANES * SEG_PITCH, LANES), _F32),
            pltpu.VMEM((1, LRU_WIDTH), _F32),
            pltpu.VMEM(((CONV_WIDTH - 1) * SUBLANES, LRU_WIDTH), _F32),
            pltpu.VMEM((tile + BLOCK, LANES), _BF16),
            pltpu.VMEM((tile + BLOCK, LANES), _BF16),
            pltpu.VMEM((tile, D_MIX), _BF16),
        ],
        compiler_params=pltpu.CompilerParams(
            dimension_semantics=("arbitrary", "arbitrary"), vmem_limit_bytes=VMEM_LIMIT_BYTES),
        name="hymba_layer",
    )(*operands)


def _mem_kv(mem, mem_norm_g, w_mem_kv, xk_norm_g):
    B = mem.shape[0]
    w_k, w_v = jnp.split(w_mem_kv, 2, axis=-1)
    w = jnp.concatenate([_permute_heads(w_k, 1), _permute_heads(w_v, 1)], axis=-1).astype(_BF16)
    head_mean = jnp.kron(jnp.eye(XATTN_HEADS, dtype=_F32),
                         jnp.full((HEAD_DIM, HEAD_DIM), 1.0 / HEAD_DIM, _F32)).astype(_BF16)
    operands = [mem, mem_norm_g.reshape(1, -1).astype(_F32), w,
                jnp.tile(xk_norm_g.astype(_F32), XATTN_HEADS).reshape(1, -1), head_mean]
    kv_shape = jax.ShapeDtypeStruct((B, MEM_LEN, XATTN_WIDTH), _BF16)
    kv_spec = pl.BlockSpec((1, MEM_LEN, XATTN_WIDTH), lambda b: (b, 0, 0))
    return pl.pallas_call(
        _mem_kv_kernel,
        out_shape=(kv_shape, kv_shape),
        grid=(B,),
        in_specs=[pl.BlockSpec((1, MEM_LEN, D_MODEL), lambda b: (b, 0, 0))]
        + [_const_spec(a.shape) for a in operands[1:]],
        out_specs=(kv_spec, kv_spec),
        compiler_params=pltpu.CompilerParams(dimension_semantics=("arbitrary",)),
        name="hymba_mem_kv",
    )(*operands)


def kernel(x, mem, norm_g, mem_norm_g, w_in, conv_w, conv_b, w_rg, b_rg, w_ig, b_ig, lru_lambda, q_norm_g,
           k_norm_g, sinks, w_mem_kv, xq_norm_g, xk_norm_g, out_norm_g, w_out):
    depth = w_in.shape[0]
    tile = TILE
    assert x.shape[1] % tile == 0 and tile % BLOCK == 0 and WINDOW == BLOCK
    h = x
    for l in range(depth):
        km, vm = _mem_kv(mem, mem_norm_g[l], w_mem_kv[l], xk_norm_g[l])
        params = dict(
            norm_g=norm_g[l], w_in=w_in[l], conv_w=conv_w[l], conv_b=conv_b[l], w_rg=w_rg[l], b_rg=b_rg[l],
            w_ig=w_ig[l], b_ig=b_ig[l], lru_lambda=lru_lambda[l], q_norm_g=q_norm_g[l], k_norm_g=k_norm_g[l],
            sinks=sinks[l], xq_norm_g=xq_norm_g[l], out_norm_g=out_norm_g[l], w_out=w_out[l])
        h = _layer(h, km, vm, params, tile=tile)
    return h
```

```python
import functools
import math

import jax
import jax.numpy as jnp
import numpy as np
from jax import lax
from jax.experimental import pallas as pl
from jax.experimental.pallas import tpu as pltpu

D_MODEL = 1024
MEM_LEN = 256
HEAD_DIM = 64
LRU_WIDTH = 512
LRU_BLOCKS = 8
LRU_BLOCK = LRU_WIDTH // LRU_BLOCKS
CONV_WIDTH = 4
LRU_C = 8.0
SWA_Q_HEADS = 4
SWA_KV_HEADS = 2
SWA_WIDTH = SWA_Q_HEADS * HEAD_DIM
SWA_KV_WIDTH = SWA_KV_HEADS * HEAD_DIM
WINDOW = 128
BLOCK = 128
XATTN_HEADS = 4
XATTN_WIDTH = XATTN_HEADS * HEAD_DIM
D_MIX = LRU_WIDTH + SWA_WIDTH + XATTN_WIDTH
ROPE_THETA = 500000.0
ROPE_DIM = HEAD_DIM // 4
EPS = 1e-6
NEG_INF = -1e30
LOG2_E = math.log2(math.e)

LANES = 128
SUBLANES = 8
MXU_DIM = 256
VMEM_LIMIT_BYTES = 56 * 1024 * 1024

TILE = 1024
SEG_LEN = TILE // SUBLANES
SEG_PITCH = SEG_LEN + SUBLANES
LRU_SLABS = LRU_WIDTH // LANES
SQRT_CLAMP = float(np.finfo(np.float32).tiny)

_P_GATE_A = 0
_P_Q = _P_GATE_A + LRU_WIDTH
_P_K = _P_Q + SWA_WIDTH
_P_V = _P_K + SWA_KV_WIDTH
_P_GATE_B = _P_V + SWA_KV_WIDTH
_P_XQ = _P_GATE_B + SWA_WIDTH
_P_GATE_C = _P_XQ + XATTN_WIDTH
_P_END = _P_GATE_C + XATTN_WIDTH

_HEAD_SLOTS = ((0, 0), (1, 0), (0, 1), (1, 1))
_HEAD_PERM = (0, 2, 1, 3)

_BF16 = jnp.bfloat16
_F32 = jnp.float32


def _dot(a, b):
    return jnp.dot(a, b, preferred_element_type=_F32)


def _dot_nt(a, b):
    return lax.dot_general(a, b, (((1,), (1,)), ((), ())), preferred_element_type=_F32)


def _sigmoid(z):
    return 0.5 * jnp.tanh(0.5 * z) + 0.5


def _silu(z):
    hz = 0.5 * z
    return hz + hz * jnp.tanh(hz)


def _sqrt_nonneg(v):
    return v * lax.rsqrt(jnp.maximum(v, SQRT_CLAMP))


def _row_rms_scale(v):
    return lax.rsqrt(jnp.mean(v * v, axis=-1, keepdims=True) + EPS)


def _head_rms_scale(v, head_mean_ref):
    sq = (v * v).astype(_BF16)
    return lax.rsqrt(_dot(sq, head_mean_ref[...]) + EPS)


def _rope(v, cos, sin):
    half = ROPE_DIM // 2
    lane = lax.broadcasted_iota(jnp.int32, v.shape, 1) % HEAD_DIM
    partner = jnp.where(lane < half, pltpu.roll(v, LANES - half, axis=1), pltpu.roll(v, half, axis=1))
    return v * cos + partner * sin


def _mem_kv_kernel(mem_ref, g_ref, w_ref, kg_ref, head_mean_ref, km_ref, vm_ref):
    m = mem_ref[0]
    mn = (m * _row_rms_scale(m) * g_ref[...]).astype(_BF16)
    kv = _dot(mn, w_ref[...])
    k = kv[:, :XATTN_WIDTH]
    k = k * _head_rms_scale(k, head_mean_ref) * kg_ref[...]
    km_ref[0] = k.astype(_BF16)
    vm_ref[0] = kv[:, XATTN_WIDTH:].astype(_BF16)


def _layer_kernel(
    sinks_ref, x_ref, km_ref, vm_ref, cos_ref, sin_ref, vec_ref, w_in_ref, w_gate_ref, head_mean_ref, w_out_ref,
    o_ref,
    proj_scr, conv_scr, h_scr, carry_scr, tail_scr, k_scr, v_scr, y_scr,
    *, tile,
):
    s = pl.program_id(1)
    n_blocks = tile // BLOCK
    norm_g_ref = vec_ref.at[0:1, :]
    out_g_ref = vec_ref.at[1:2, :]
    conv_b_ref = vec_ref.at[2:3, 0:LRU_WIDTH]
    lam_ref = vec_ref.at[2:3, LRU_WIDTH:]
    b_rg_ref = vec_ref.at[3:4, 0:LRU_WIDTH]
    b_ig_ref = vec_ref.at[3:4, LRU_WIDTH:]
    conv_w_ref = vec_ref.at[4:4 + CONV_WIDTH, 0:LRU_WIDTH]
    qg_ref = vec_ref.at[8:9, 0:LANES]
    kg_ref = vec_ref.at[8:9, LANES:2 * LANES]
    xqg_ref = vec_ref.at[8:9, 2 * LANES:3 * LANES]

    @pl.when(s == 0)
    def _reset_sequence_state():
        tail_scr[...] = jnp.zeros_like(tail_scr)
        carry_scr[...] = jnp.zeros_like(carry_scr)
        k_scr[tile:tile + BLOCK, :] = jnp.zeros((BLOCK, LANES), _BF16)
        v_scr[tile:tile + BLOCK, :] = jnp.zeros((BLOCK, LANES), _BF16)

    k_scr[0:BLOCK, :] = k_scr[tile:tile + BLOCK, :]
    v_scr[0:BLOCK, :] = v_scr[tile:tile + BLOCK, :]

    x = x_ref[0]
    xn = (x * _row_rms_scale(x) * norm_g_ref[...]).astype(_BF16)
    lru_x = _dot(xn, w_in_ref[:, 0:LRU_WIDTH])
    for seg in range(SUBLANES):
        for slab in range(LRU_SLABS):
            conv_scr[slab, seg * SEG_PITCH:seg * SEG_PITCH + SEG_LEN, :] = (
                lru_x[seg * SEG_LEN:(seg + 1) * SEG_LEN, slab * LANES:(slab + 1) * LANES])
    proj_scr[...] = _dot(xn, w_in_ref[:, LRU_WIDTH:])

    seg_id = lax.broadcasted_iota(jnp.int32, (SUBLANES, LRU_WIDTH), 0)

    def strided(ref, g):
        return jnp.concatenate(
            [ref[slab, pl.ds(g, SUBLANES, stride=SEG_PITCH), :] for slab in range(LRU_SLABS)], axis=-1)

    def shift_segments(cur, prev_tile):
        return jnp.where(seg_id == 0, pltpu.roll(prev_tile, 1, axis=0), pltpu.roll(cur, 1, axis=0))

    u = [strided(conv_scr, g) for g in range(SEG_LEN)]
    wrapped = {}
    for k in range(1, CONV_WIDTH):
        wrapped[SEG_LEN - k] = shift_segments(u[SEG_LEN - k], tail_scr[(k - 1) * SUBLANES:k * SUBLANES, :])
    for k in range(1, CONV_WIDTH):
        tail_scr[(k - 1) * SUBLANES:k * SUBLANES, :] = u[SEG_LEN - k]
    xc_parts = []
    for g in range(SEG_LEN):
        xc = conv_b_ref[...] + conv_w_ref[CONV_WIDTH - 1:CONV_WIDTH, :] * u[g]
        for k in range(1, CONV_WIDTH):
            past = u[g - k] if g >= k else wrapped[SEG_LEN + g - k]
            xc = xc + conv_w_ref[CONV_WIDTH - 1 - k:CONV_WIDTH - k, :] * past
        xc_parts.append(xc)
    xc = jnp.concatenate(xc_parts, axis=0)

    lam = -lam_ref[...]
    softplus = jnp.maximum(lam, 0.0) + jnp.log1p(jnp.exp(-jnp.abs(lam)))
    neg_c_softplus = -LRU_C * softplus
    a_parts, b_parts = [], []
    for j in range(LRU_WIDTH // MXU_DIM):
        cols = slice(j * MXU_DIM, (j + 1) * MXU_DIM)
        pre = _dot(xc[:, cols].astype(_BF16), w_gate_ref[j])
        rg = _sigmoid(pre[:, :MXU_DIM] + b_rg_ref[:, cols])
        ig = _sigmoid(pre[:, MXU_DIM:] + b_ig_ref[:, cols])
        log_a = rg * neg_c_softplus[:, cols]
        a = jnp.exp(log_a)
        a_parts.append(a)
        b_parts.append(_sqrt_nonneg(jnp.tanh(-log_a) * (1.0 + a * a)) * (ig * xc[:, cols]))
    a_all = jnp.concatenate(a_parts, axis=-1)
    b_all = jnp.concatenate(b_parts, axis=-1)

    local, decay = [], []
    for g in range(SEG_LEN):
        a_g = a_all[g * SUBLANES:(g + 1) * SUBLANES, :]
        b_g = b_all[g * SUBLANES:(g + 1) * SUBLANES, :]
        local.append(b_g if g == 0 else a_g * local[-1] + b_g)
        decay.append(a_g if g == 0 else a_g * decay[-1])
    seg_a, seg_b = decay[-1], local[-1]
    d = 1
    while d < SUBLANES:
        keep = seg_id >= d
        a_prev = jnp.where(keep, pltpu.roll(seg_a, d, axis=0), 1.0)
        b_prev = jnp.where(keep, pltpu.roll(seg_b, d, axis=0), 0.0)
        seg_b = seg_b + seg_a * b_prev
        seg_a = seg_a * a_prev
        d *= 2
    carry = carry_scr[...]
    seg_end = seg_a * carry + seg_b
    seg_in = jnp.where(seg_id == 0, carry, pltpu.roll(seg_end, 1, axis=0))
    carry_scr[...] = seg_end[SUBLANES - 1:SUBLANES, :]
    for g in range(SEG_LEN):
        h_g = local[g] + decay[g] * seg_in
        for slab in range(LRU_SLABS):
            h_scr[slab, pl.ds(g, SUBLANES, stride=SEG_PITCH), :] = h_g[:, slab * LANES:(slab + 1) * LANES]

    h = jnp.concatenate([
        jnp.concatenate([h_scr[slab, seg * SEG_PITCH:seg * SEG_PITCH + SEG_LEN, :] for slab in range(LRU_SLABS)],
                        axis=-1)
        for seg in range(SUBLANES)], axis=0)
    gate_a = proj_scr[:, _P_GATE_A:_P_GATE_A + LRU_WIDTH]
    y_a = h * _row_rms_scale(h) * out_g_ref[:, 0:LRU_WIDTH] * _silu(gate_a)
    y_scr[:, 0:LRU_WIDTH] = y_a.astype(_BF16)

    pos = pl.ds(pl.multiple_of(s * tile, tile), tile)
    cos = cos_ref[pos, :]
    sin = sin_ref[pos, :]
    score_scale = LOG2_E / math.sqrt(HEAD_DIM)

    q_groups = []
    for g in range(SWA_WIDTH // LANES):
        cols = slice(_P_Q + g * LANES, _P_Q + (g + 1) * LANES)
        qraw = proj_scr[:, cols]
        qn = qraw * _head_rms_scale(qraw, head_mean_ref) * qg_ref[...]
        q_groups.append(_rope(qn, cos, sin) * score_scale)
    kraw = proj_scr[:, _P_K:_P_K + SWA_KV_WIDTH]
    kn = kraw * _head_rms_scale(kraw, head_mean_ref) * kg_ref[...]
    k_scr[BLOCK:BLOCK + tile, :] = _rope(kn, cos, sin).astype(_BF16)
    v_scr[BLOCK:BLOCK + tile, :] = proj_scr[:, _P_V:_P_V + SWA_KV_WIDTH].astype(_BF16)

    xq_groups = []
    for g in range(XATTN_WIDTH // LANES):
        cols = slice(_P_XQ + g * LANES, _P_XQ + (g + 1) * LANES)
        xraw = proj_scr[:, cols]
        xq_groups.append(xraw * _head_rms_scale(xraw, head_mean_ref) * xqg_ref[...] * score_scale)

    lane = lax.broadcasted_iota(jnp.int32, (BLOCK, LANES), 1)
    low_half = lane < HEAD_DIM
    half_masks = (low_half, lane >= HEAD_DIM)
    qi = lax.broadcasted_iota(jnp.int32, (BLOCK, BLOCK), 0)
    kj = lax.broadcasted_iota(jnp.int32, (BLOCK, BLOCK), 1)
    own_block = kj <= qi

    for blk in range(n_blocks):
        rows = slice(blk * BLOCK, (blk + 1) * BLOCK)
        kk = k_scr[blk * BLOCK:(blk + 2) * BLOCK, :]
        vv = v_scr[blk * BLOCK:(blk + 2) * BLOCK, :]
        o_swa = [None, None]
        for head, (g, hi) in enumerate(_HEAD_SLOTS):
            qm = jnp.where(half_masks[hi], q_groups[g][rows, :], 0.0).astype(_BF16)
            sc_both = _dot_nt(qm, kk)
            sc_prev = sc_both[:, :BLOCK]
            if blk == 0:
                sc_prev = jnp.where(s > 0, sc_prev, NEG_INF)
            sc = jnp.where(own_block, sc_both[:, BLOCK:], sc_prev)
            sink = sinks_ref[head] * LOG2_E
            m = jnp.maximum(jnp.max(sc, axis=-1, keepdims=True), sink)
            p = jnp.exp2(sc - m)
            den = jnp.sum(p, axis=-1, keepdims=True) + jnp.exp2(sink - m)
            p_both = jnp.concatenate([jnp.where(own_block, 0.0, p), jnp.where(own_block, p, 0.0)], axis=-1)
            o = _dot(p_both.astype(_BF16), vv) * (1.0 / den)
            o_swa[g] = o if o_swa[g] is None else jnp.where(low_half, o_swa[g], o)
        y_b = jnp.concatenate(o_swa, axis=-1)
        gate_b = proj_scr[rows, _P_GATE_B:_P_GATE_B + SWA_WIDTH]
        y_b = y_b * _row_rms_scale(y_b) * out_g_ref[:, LRU_WIDTH:LRU_WIDTH + SWA_WIDTH] * _silu(gate_b)
        y_scr[rows, LRU_WIDTH:LRU_WIDTH + SWA_WIDTH] = y_b.astype(_BF16)

        o_mem = [None, None]
        for head, (g, hi) in enumerate(_HEAD_SLOTS):
            qm = jnp.where(half_masks[hi], xq_groups[g][rows, :], 0.0).astype(_BF16)
            sc = _dot_nt(qm, km_ref[0, :, g * LANES:(g + 1) * LANES])
            m = jnp.max(sc, axis=-1, keepdims=True)
            p = jnp.exp2(sc - m)
            den = jnp.sum(p, axis=-1, keepdims=True)
            o = _dot(p.astype(_BF16), vm_ref[0, :, g * LANES:(g + 1) * LANES]) * (1.0 / den)
            o_mem[g] = o if o_mem[g] is None else jnp.where(low_half, o_mem[g], o)
        y_c = jnp.concatenate(o_mem, axis=-1)
        gate_c = proj_scr[rows, _P_GATE_C:_P_GATE_C + XATTN_WIDTH]
        y_c = y_c * _row_rms_scale(y_c) * out_g_ref[:, LRU_WIDTH + SWA_WIDTH:] * _silu(gate_c)
        y_scr[rows, LRU_WIDTH + SWA_WIDTH:] = y_c.astype(_BF16)

    o_ref[0] = x + _dot(y_scr[...], w_out_ref[...])


def _permute_heads(w, axis):
    heads = [lax.slice_in_dim(w, h * HEAD_DIM, (h + 1) * HEAD_DIM, axis=axis) for h in _HEAD_PERM]
    return jnp.concatenate(heads, axis=axis)


def _rope_tables(seq):
    pos = np.arange(seq, dtype=np.float32)
    inv_freq = np.float32(ROPE_THETA) ** (-(np.arange(0, ROPE_DIM, 2, dtype=np.float32) / np.float32(ROPE_DIM)))
    ang = (pos[:, None] * inv_freq[None, :].astype(np.float32)).astype(np.float32)
    cos, sin = np.cos(ang).astype(np.float32), np.sin(ang).astype(np.float32)
    ones = np.ones((seq, HEAD_DIM - ROPE_DIM), np.float32)
    cos_head = np.concatenate([cos, cos, ones], axis=-1)
    sin_head = np.concatenate([-sin, sin, 0.0 * ones], axis=-1)
    reps = LANES // HEAD_DIM
    return jnp.asarray(np.tile(cos_head, (1, reps))), jnp.asarray(np.tile(sin_head, (1, reps)))


def _head_mean_matrix(width):
    block = np.full((HEAD_DIM, HEAD_DIM), 1.0 / HEAD_DIM, np.float32)
    return jnp.asarray(np.kron(np.eye(width // HEAD_DIM, dtype=np.float32), block), dtype=_BF16)


def _pack_vectors(p, out_g):
    f32 = lambda v: v.astype(_F32)
    tile_heads = lambda g: jnp.tile(f32(g), LANES // HEAD_DIM)
    pad = lambda v: jnp.pad(v, (0, D_MODEL - v.shape[0]))
    rows = [
        f32(p["norm_g"]),
        f32(out_g),
        jnp.concatenate([f32(p["conv_b"]), f32(p["lru_lambda"])]),
        jnp.concatenate([f32(p["b_rg"]), f32(p["b_ig"])]),
    ] + [pad(f32(p["conv_w"][j])) for j in range(CONV_WIDTH)] + [
        pad(jnp.concatenate([tile_heads(p["q_norm_g"]), tile_heads(p["k_norm_g"]), tile_heads(p["xq_norm_g"])])),
    ]
    packed = jnp.stack(rows)
    return jnp.pad(packed, ((0, 2 * SUBLANES - packed.shape[0]), (0, 0)))


def _const_spec(shape):
    return pl.BlockSpec(shape, lambda *_: (0,) * len(shape))


def _layer(x, km, vm, p, *, tile):
    B, S, _ = x.shape
    n_tiles = S // tile
    lru_x, gate_a, sq, sk, sv, gate_b, xq, gate_c = jnp.split(
        p["w_in"], list(np.cumsum((LRU_WIDTH, LRU_WIDTH, SWA_WIDTH, SWA_KV_WIDTH, SWA_KV_WIDTH, SWA_WIDTH,
                                    XATTN_WIDTH))), axis=-1)
    w_in = jnp.concatenate(
        [lru_x, gate_a, _permute_heads(sq, 1), sk, sv, _permute_heads(gate_b, 1), _permute_heads(xq, 1),
         _permute_heads(gate_c, 1)], axis=-1).astype(_BF16)

    def block_diag(w):
        eye = jnp.eye(LRU_BLOCKS, dtype=w.dtype)
        return jnp.einsum("nde,nm->ndme", w, eye).reshape(LRU_WIDTH, LRU_WIDTH)

    w_r, w_i = block_diag(p["w_rg"]), block_diag(p["w_ig"])
    w_gate = jnp.stack([
        jnp.concatenate([w_r[c:c + MXU_DIM, c:c + MXU_DIM], w_i[c:c + MXU_DIM, c:c + MXU_DIM]], axis=-1)
        for c in range(0, LRU_WIDTH, MXU_DIM)]).astype(_BF16)

    g_a, g_b, g_c = jnp.split(p["out_norm_g"], [LRU_WIDTH, LRU_WIDTH + SWA_WIDTH])
    out_g = jnp.concatenate([g_a, _permute_heads(g_b, 0), _permute_heads(g_c, 0)])
    w_a, w_b, w_c = jnp.split(p["w_out"], [LRU_WIDTH, LRU_WIDTH + SWA_WIDTH], axis=0)
    w_out = jnp.concatenate([w_a, _permute_heads(w_b, 0), _permute_heads(w_c, 0)], axis=0).astype(_BF16)

    cos, sin = _rope_tables(S)
    operands = [
        p["sinks"].astype(_F32),
        x, km, vm, cos, sin, _pack_vectors(p, out_g), w_in, w_gate, _head_mean_matrix(LANES), w_out,
    ]
    in_specs = [
        pl.BlockSpec(memory_space=pltpu.SMEM),
        pl.BlockSpec((1, tile, D_MODEL), lambda b, s: (b, s, 0)),
        pl.BlockSpec((1, MEM_LEN, XATTN_WIDTH), lambda b, s: (b, 0, 0)),
        pl.BlockSpec((1, MEM_LEN, XATTN_WIDTH), lambda b, s: (b, 0, 0)),
    ] + [_const_spec(a.shape) for a in operands[4:]]

    return pl.pallas_call(
        functools.partial(_layer_kernel, tile=tile),
        out_shape=jax.ShapeDtypeStruct(x.shape, x.dtype),
        grid=(B, n_tiles),
        in_specs=in_specs,
        out_specs=pl.BlockSpec((1, tile, D_MODEL), lambda b, s: (b, s, 0)),
        scratch_shapes=[
            pltpu.VMEM((tile, _P_END), _F32),
            pltpu.VMEM((LRU_SLABS, SUBLANES * SEG_PITCH, LANES), _F32),
            pltpu.VMEM((LRU_SLABS, SUBLANES * SEG_PITCH, LANES), _F32),
            pltpu.VMEM((1, LRU_WIDTH), _F32),
            pltpu.VMEM(((CONV_WIDTH - 1) * SUBLANES, LRU_WIDTH), _F32),
            pltpu.VMEM((tile + BLOCK, LANES), _BF16),
            pltpu.VMEM((tile + BLOCK, LANES), _BF16),
            pltpu.VMEM((tile, D_MIX), _BF16),
        ],
        compiler_params=pltpu.CompilerParams(
            dimension_semantics=("arbitrary", "arbitrary"), vmem_limit_bytes=VMEM_LIMIT_BYTES),
        name="hymba_layer",
    )(*operands)


def _mem_kv(mem, mem_norm_g, w_mem_kv, xk_norm_g):
    B = mem.shape[0]
    w_k, w_v = jnp.split(w_mem_kv, 2, axis=-1)
    w = jnp.concatenate([_permute_heads(w_k, 1), _permute_heads(w_v, 1)], axis=-1).astype(_BF16)
    operands = [mem, mem_norm_g.reshape(1, -1).astype(_F32), w,
                jnp.tile(xk_norm_g.astype(_F32), XATTN_HEADS).reshape(1, -1), _head_mean_matrix(XATTN_WIDTH)]
    kv_shape = jax.ShapeDtypeStruct((B, MEM_LEN, XATTN_WIDTH), _BF16)
    kv_spec = pl.BlockSpec((1, MEM_LEN, XATTN_WIDTH), lambda b: (b, 0, 0))
    return pl.pallas_call(
        _mem_kv_kernel,
        out_shape=(kv_shape, kv_shape),
        grid=(B,),
        in_specs=[pl.BlockSpec((1, MEM_LEN, D_MODEL), lambda b: (b, 0, 0))]
        + [_const_spec(a.shape) for a in operands[1:]],
        out_specs=(kv_spec, kv_spec),
        compiler_params=pltpu.CompilerParams(dimension_semantics=("arbitrary",)),
        name="hymba_mem_kv",
    )(*operands)


def kernel(x, mem, norm_g, mem_norm_g, w_in, conv_w, conv_b, w_rg, b_rg, w_ig, b_ig, lru_lambda, q_norm_g,
           k_norm_g, sinks, w_mem_kv, xq_norm_g, xk_norm_g, out_norm_g, w_out):
    depth = w_in.shape[0]
    tile = TILE
    assert x.shape[1] % tile == 0 and tile % BLOCK == 0 and WINDOW == BLOCK
    h = x
    for l in range(depth):
        km, vm = _mem_kv(mem, mem_norm_g[l], w_mem_kv[l], xk_norm_g[l])
        params = dict(
            norm_g=norm_g[l], w_in=w_in[l], conv_w=conv_w[l], conv_b=conv_b[l], w_rg=w_rg[l], b_rg=b_rg[l],
            w_ig=w_ig[l], b_ig=b_ig[l], lru_lambda=lru_lambda[l], q_norm_g=q_norm_g[l], k_norm_g=k_norm_g[l],
            sinks=sinks[l], xq_norm_g=xq_norm_g[l], out_norm_g=out_norm_g[l], w_out=w_out[l])
        h = _layer(h, km, vm, params, tile=tile)
    return h
```

```python
import functools
import math

import jax
import jax.numpy as jnp
import numpy as np
from jax import lax
from jax.experimental import pallas as pl
from jax.experimental.pallas import tpu as pltpu

D_MODEL = 1024
MEM_LEN = 256
HEAD_DIM = 64
LRU_WIDTH = 512
LRU_BLOCKS = 8
LRU_BLOCK = LRU_WIDTH // LRU_BLOCKS
CONV_WIDTH = 4
LRU_C = 8.0
SWA_Q_HEADS = 4
SWA_KV_HEADS = 2
SWA_WIDTH = SWA_Q_HEADS * HEAD_DIM
SWA_KV_WIDTH = SWA_KV_HEADS * HEAD_DIM
WINDOW = 128
BLOCK = 128
XATTN_HEADS = 4
XATTN_WIDTH = XATTN_HEADS * HEAD_DIM
D_MIX = LRU_WIDTH + SWA_WIDTH + XATTN_WIDTH
ROPE_THETA = 500000.0
ROPE_DIM = HEAD_DIM // 4
EPS = 1e-6
NEG_INF = -1e30
LOG2_E = math.log2(math.e)

LANES = 128
SUBLANES = 8
MXU_DIM = 256
VMEM_LIMIT_BYTES = 56 * 1024 * 1024

TILE = 1024
SEG_LEN = TILE // SUBLANES
SEG_PITCH = SEG_LEN + SUBLANES
LRU_SLABS = LRU_WIDTH // LANES
SQRT_CLAMP = float(np.finfo(np.float32).tiny)

_P_GATE_A = 0
_P_Q = _P_GATE_A + LRU_WIDTH
_P_K = _P_Q + SWA_WIDTH
_P_V = _P_K + SWA_KV_WIDTH
_P_GATE_B = _P_V + SWA_KV_WIDTH
_P_XQ = _P_GATE_B + SWA_WIDTH
_P_GATE_C = _P_XQ + XATTN_WIDTH
_P_END = _P_GATE_C + XATTN_WIDTH

_HEAD_SLOTS = ((0, 0), (0, 1), (1, 0), (1, 1))

_BF16 = jnp.bfloat16
_F32 = jnp.float32


def _dot(a, b):
    return jnp.dot(a, b, preferred_element_type=_F32)


def _dot_nt(a, b):
    return lax.dot_general(a, b, (((1,), (1,)), ((), ())), preferred_element_type=_F32)


def _sigmoid_of_half(hz):
    return 0.5 * jnp.tanh(hz) + 0.5


def _silu_of_half(hz):
    return hz + hz * jnp.tanh(hz)


def _sqrt_nonneg(v):
    return v * lax.rsqrt(jnp.maximum(v, SQRT_CLAMP))


def _row_rms_scale(v):
    return lax.rsqrt(jnp.mean(v * v, axis=-1, keepdims=True) + EPS)


def _head_rms_scale(v, head_mean_ref):
    sq = (v * v).astype(_BF16)
    return lax.rsqrt(_dot(sq, head_mean_ref[...]) + EPS)


def _rope(v, cos, sin):
    half = ROPE_DIM // 2
    lane = lax.broadcasted_iota(jnp.int32, v.shape, 1) % HEAD_DIM
    partner = jnp.where(lane < half, pltpu.roll(v, LANES - half, axis=1), pltpu.roll(v, half, axis=1))
    return v * cos + partner * sin


def _mem_kv_kernel(mem_ref, g_ref, w_ref, kg_ref, head_mean_ref, km_ref, vm_ref):
    m = mem_ref[0]
    mn = (m * _row_rms_scale(m) * g_ref[...]).astype(_BF16)
    kv = _dot(mn, w_ref[...])
    k = kv[:, :XATTN_WIDTH]
    k = k * _head_rms_scale(k, head_mean_ref) * jnp.concatenate([kg_ref[...]] * XATTN_HEADS, axis=1)
    km_ref[0] = k.astype(_BF16)
    vm_ref[0] = kv[:, XATTN_WIDTH:].astype(_BF16)


def _layer_kernel(
    sinks_ref, x_ref, km_ref, vm_ref, cos_ref, sin_ref, norm_g_ref, out_g_ref, conv_w3_ref, conv_b_ref, b_rg_ref,
    b_ig_ref, lam_ref, qg_ref, kg_ref, xqg_ref, w_in_ref, w_gate_ref, head_mean_ref, w_out_ref,
    o_ref,
    proj_scr, conv_scr, h_scr, carry_scr, tail_scr, k_scr, v_scr, y_scr,
    *, tile, layer,
):
    s = pl.program_id(1)
    n_blocks = tile // BLOCK
    conv_w_ref = conv_w3_ref.at[0]
    score_scale = LOG2_E / math.sqrt(HEAD_DIM)
    two_heads = lambda g_ref: jnp.concatenate([g_ref[...]] * (LANES // HEAD_DIM), axis=1)
    q_gain = two_heads(qg_ref) * score_scale
    k_gain = two_heads(kg_ref)
    xq_gain = two_heads(xqg_ref) * score_scale

    @pl.when(s == 0)
    def _reset_sequence_state():
        tail_scr[...] = jnp.zeros_like(tail_scr)
        carry_scr[...] = jnp.zeros_like(carry_scr)
        k_scr[:, tile:tile + BLOCK, :] = jnp.zeros((SWA_KV_HEADS, BLOCK, LANES), _BF16)
        v_scr[:, tile:tile + BLOCK, :] = jnp.zeros((SWA_KV_HEADS, BLOCK, LANES), _BF16)

    k_scr[:, 0:BLOCK, :] = k_scr[:, tile:tile + BLOCK, :]
    v_scr[:, 0:BLOCK, :] = v_scr[:, tile:tile + BLOCK, :]

    x = x_ref[0]
    xn = (x * _row_rms_scale(x) * norm_g_ref[...]).astype(_BF16)
    lru_x = _dot(xn, w_in_ref[:, 0:LRU_WIDTH])
    for seg in range(SUBLANES):
        for slab in range(LRU_SLABS):
            conv_scr[slab, seg * SEG_PITCH:seg * SEG_PITCH + SEG_LEN, :] = (
                lru_x[seg * SEG_LEN:(seg + 1) * SEG_LEN, slab * LANES:(slab + 1) * LANES])
    proj_scr[...] = _dot(xn, w_in_ref[:, LRU_WIDTH:])

    seg_id = lax.broadcasted_iota(jnp.int32, (SUBLANES, LRU_WIDTH), 0)

    def strided(ref, g):
        return jnp.concatenate(
            [ref[slab, pl.ds(g, SUBLANES, stride=SEG_PITCH), :] for slab in range(LRU_SLABS)], axis=-1)

    def shift_segments(cur, prev_tile):
        return jnp.where(seg_id == 0, pltpu.roll(prev_tile, 1, axis=0), pltpu.roll(cur, 1, axis=0))

    u = [strided(conv_scr, g) for g in range(SEG_LEN)]
    wrapped = {}
    for k in range(1, CONV_WIDTH):
        wrapped[SEG_LEN - k] = shift_segments(u[SEG_LEN - k], tail_scr[(k - 1) * SUBLANES:k * SUBLANES, :])
    for k in range(1, CONV_WIDTH):
        tail_scr[(k - 1) * SUBLANES:k * SUBLANES, :] = u[SEG_LEN - k]
    xc_parts = []
    for g in range(SEG_LEN):
        xc = conv_b_ref[...] + conv_w_ref[CONV_WIDTH - 1:CONV_WIDTH, :] * u[g]
        for k in range(1, CONV_WIDTH):
            past = u[g - k] if g >= k else wrapped[SEG_LEN + g - k]
            xc = xc + conv_w_ref[CONV_WIDTH - 1 - k:CONV_WIDTH - k, :] * past
        xc_parts.append(xc)
    xc = jnp.concatenate(xc_parts, axis=0)

    lam = -lam_ref[...]
    softplus = jnp.maximum(lam, 0.0) + jnp.log1p(jnp.exp(-jnp.abs(lam)))
    neg_c_softplus = -LRU_C * softplus
    a_parts, b_parts = [], []
    for j in range(LRU_WIDTH // MXU_DIM):
        cols = slice(j * MXU_DIM, (j + 1) * MXU_DIM)
        pre = _dot(xc[:, cols].astype(_BF16), w_gate_ref[j])
        rg = _sigmoid_of_half(pre[:, :MXU_DIM] + 0.5 * b_rg_ref[:, cols])
        ig = _sigmoid_of_half(pre[:, MXU_DIM:] + 0.5 * b_ig_ref[:, cols])
        log_a = rg * neg_c_softplus[:, cols]
        a = jnp.exp(log_a)
        a_parts.append(a)
        b_parts.append(_sqrt_nonneg(jnp.tanh(-log_a) * (1.0 + a * a)) * (ig * xc[:, cols]))
    a_all = jnp.concatenate(a_parts, axis=-1)
    b_all = jnp.concatenate(b_parts, axis=-1)

    local, decay = [], []
    for g in range(SEG_LEN):
        a_g = a_all[g * SUBLANES:(g + 1) * SUBLANES, :]
        b_g = b_all[g * SUBLANES:(g + 1) * SUBLANES, :]
        local.append(b_g if g == 0 else a_g * local[-1] + b_g)
        decay.append(a_g if g == 0 else a_g * decay[-1])
    seg_a, seg_b = decay[-1], local[-1]
    d = 1
    while d < SUBLANES:
        keep = seg_id >= d
        a_prev = jnp.where(keep, pltpu.roll(seg_a, d, axis=0), 1.0)
        b_prev = jnp.where(keep, pltpu.roll(seg_b, d, axis=0), 0.0)
        seg_b = seg_b + seg_a * b_prev
        seg_a = seg_a * a_prev
        d *= 2
    carry = carry_scr[...]
    seg_end = seg_a * carry + seg_b
    seg_in = jnp.where(seg_id == 0, carry, pltpu.roll(seg_end, 1, axis=0))
    carry_scr[...] = seg_end[SUBLANES - 1:SUBLANES, :]
    for g in range(SEG_LEN):
        h_g = local[g] + decay[g] * seg_in
        for slab in range(LRU_SLABS):
            h_scr[slab, pl.ds(g, SUBLANES, stride=SEG_PITCH), :] = h_g[:, slab * LANES:(slab + 1) * LANES]

    h = jnp.concatenate([
        jnp.concatenate([h_scr[slab, seg * SEG_PITCH:seg * SEG_PITCH + SEG_LEN, :] for slab in range(LRU_SLABS)],
                        axis=-1)
        for seg in range(SUBLANES)], axis=0)
    gate_a = proj_scr[:, _P_GATE_A:_P_GATE_A + LRU_WIDTH]
    y_a = h * _row_rms_scale(h) * out_g_ref[:, 0:LRU_WIDTH] * _silu_of_half(gate_a)
    y_scr[:, 0:LRU_WIDTH] = y_a.astype(_BF16)

    pos = pl.ds(pl.multiple_of(s * tile, tile), tile)
    cos = cos_ref[pos, :]
    sin = sin_ref[pos, :]
    q_groups = []
    for g in range(SWA_WIDTH // LANES):
        cols = slice(_P_Q + g * LANES, _P_Q + (g + 1) * LANES)
        qraw = proj_scr[:, cols]
        qn = qraw * _head_rms_scale(qraw, head_mean_ref) * q_gain
        q_groups.append(_rope(qn, cos, sin))
    kraw = proj_scr[:, _P_K:_P_K + SWA_KV_WIDTH]
    kn = kraw * _head_rms_scale(kraw, head_mean_ref) * k_gain
    lane_t = lax.broadcasted_iota(jnp.int32, (tile, LANES), 1)
    for name_scr, kv in ((k_scr, _rope(kn, cos, sin)), (v_scr, proj_scr[:, _P_V:_P_V + SWA_KV_WIDTH])):
        swapped = pltpu.roll(kv, HEAD_DIM, axis=1)
        name_scr[0, BLOCK:BLOCK + tile, :] = jnp.where(lane_t < HEAD_DIM, kv, swapped).astype(_BF16)
        name_scr[1, BLOCK:BLOCK + tile, :] = jnp.where(lane_t < HEAD_DIM, swapped, kv).astype(_BF16)

    xq_groups = []
    for g in range(XATTN_WIDTH // LANES):
        cols = slice(_P_XQ + g * LANES, _P_XQ + (g + 1) * LANES)
        xraw = proj_scr[:, cols]
        xq_groups.append(xraw * _head_rms_scale(xraw, head_mean_ref) * xq_gain)

    lane = lax.broadcasted_iota(jnp.int32, (BLOCK, LANES), 1)
    low_half = lane < HEAD_DIM
    half_masks = (low_half, lane >= HEAD_DIM)
    qi = lax.broadcasted_iota(jnp.int32, (BLOCK, BLOCK), 0)
    kj = lax.broadcasted_iota(jnp.int32, (BLOCK, BLOCK), 1)
    own_block = kj <= qi

    for blk in range(n_blocks):
        rows = slice(blk * BLOCK, (blk + 1) * BLOCK)
        o_swa = [None, None]
        for head, (g, hi) in enumerate(_HEAD_SLOTS):
            kv_head = head // (SWA_Q_HEADS // SWA_KV_HEADS)
            kk = k_scr[kv_head, blk * BLOCK:(blk + 2) * BLOCK, :]
            vv = v_scr[kv_head, blk * BLOCK:(blk + 2) * BLOCK, :]
            qm = jnp.where(half_masks[hi], q_groups[g][rows, :], 0.0).astype(_BF16)
            sc_both = _dot_nt(qm, kk)
            sc_prev = sc_both[:, :BLOCK]
            if blk == 0:
                sc_prev = jnp.where(s > 0, sc_prev, NEG_INF)
            sc = jnp.where(own_block, sc_both[:, BLOCK:], sc_prev)
            sink = sinks_ref[layer, head] * LOG2_E
            m = jnp.maximum(jnp.max(sc, axis=-1, keepdims=True), sink)
            p = jnp.exp2(sc - m)
            den = jnp.sum(p, axis=-1, keepdims=True) + jnp.exp2(sink - m)
            p_both = jnp.concatenate([jnp.where(own_block, 0.0, p), jnp.where(own_block, p, 0.0)], axis=-1)
            o = _dot(p_both.astype(_BF16), vv) * (1.0 / den)
            o_swa[g] = o if o_swa[g] is None else jnp.where(low_half, o_swa[g], o)
        y_b = jnp.concatenate(o_swa, axis=-1)
        gate_b = proj_scr[rows, _P_GATE_B:_P_GATE_B + SWA_WIDTH]
        y_b = y_b * _row_rms_scale(y_b) * out_g_ref[:, LRU_WIDTH:LRU_WIDTH + SWA_WIDTH] * _silu_of_half(gate_b)
        y_scr[rows, LRU_WIDTH:LRU_WIDTH + SWA_WIDTH] = y_b.astype(_BF16)

        o_mem = [None, None]
        for head, (g, hi) in enumerate(_HEAD_SLOTS):
            qm = jnp.where(half_masks[hi], xq_groups[g][rows, :], 0.0).astype(_BF16)
            sc = _dot_nt(qm, km_ref[0, :, g * LANES:(g + 1) * LANES])
            m = jnp.max(sc, axis=-1, keepdims=True)
            p = jnp.exp2(sc - m)
            den = jnp.sum(p, axis=-1, keepdims=True)
            o = _dot(p.astype(_BF16), vm_ref[0, :, g * LANES:(g + 1) * LANES]) * (1.0 / den)
            o_mem[g] = o if o_mem[g] is None else jnp.where(low_half, o_mem[g], o)
        y_c = jnp.concatenate(o_mem, axis=-1)
        gate_c = proj_scr[rows, _P_GATE_C:_P_GATE_C + XATTN_WIDTH]
        y_c = y_c * _row_rms_scale(y_c) * out_g_ref[:, LRU_WIDTH + SWA_WIDTH:] * _silu_of_half(gate_c)
        y_scr[rows, LRU_WIDTH + SWA_WIDTH:] = y_c.astype(_BF16)

    o_ref[0] = x + _dot(y_scr[...], w_out_ref[...])


def _rope_tables(seq):
    pos = np.arange(seq, dtype=np.float32)
    inv_freq = np.float32(ROPE_THETA) ** (-(np.arange(0, ROPE_DIM, 2, dtype=np.float32) / np.float32(ROPE_DIM)))
    ang = (pos[:, None] * inv_freq[None, :].astype(np.float32)).astype(np.float32)
    cos, sin = np.cos(ang).astype(np.float32), np.sin(ang).astype(np.float32)
    ones = np.ones((seq, HEAD_DIM - ROPE_DIM), np.float32)
    cos_head = np.concatenate([cos, cos, ones], axis=-1)
    sin_head = np.concatenate([-sin, sin, 0.0 * ones], axis=-1)
    reps = LANES // HEAD_DIM
    return jnp.asarray(np.tile(cos_head, (1, reps))), jnp.asarray(np.tile(sin_head, (1, reps)))


def _head_mean_matrix(width):
    block = np.full((HEAD_DIM, HEAD_DIM), 1.0 / HEAD_DIM, np.float32)
    return jnp.asarray(np.kron(np.eye(width // HEAD_DIM, dtype=np.float32), block), dtype=_BF16)


def _const_spec(shape):
    return pl.BlockSpec(shape, lambda *_: (0,) * len(shape))


def _gate_half_columns():
    scale = np.ones((LRU_WIDTH + _P_END,), np.float32)
    for start, width in ((_P_GATE_A, LRU_WIDTH), (_P_GATE_B, SWA_WIDTH), (_P_GATE_C, XATTN_WIDTH)):
        scale[LRU_WIDTH + start:LRU_WIDTH + start + width] = 0.5
    return scale


def _layer(x, km, vm, p, layer, *, tile):
    B, S, _ = x.shape
    n_tiles = S // tile
    w_in = (p["w_in"][layer] * _gate_half_columns()).astype(_BF16)
    w_out = p["w_out"][layer].astype(_BF16)

    def block_diag(w):
        eye = jnp.eye(LRU_BLOCKS, dtype=w.dtype)
        return jnp.einsum("nde,nm->ndme", w, eye).reshape(LRU_WIDTH, LRU_WIDTH)

    w_r, w_i = block_diag(p["w_rg"][layer]), block_diag(p["w_ig"][layer])
    w_gate = (0.5 * jnp.stack([
        jnp.concatenate([w_r[c:c + MXU_DIM, c:c + MXU_DIM], w_i[c:c + MXU_DIM, c:c + MXU_DIM]], axis=-1)
        for c in range(0, LRU_WIDTH, MXU_DIM)])).astype(_BF16)

    cos, sin = _rope_tables(S)
    vector_names = ("norm_g", "out_norm_g", "conv_w", "conv_b", "b_rg", "b_ig", "lru_lambda",
                    "q_norm_g", "k_norm_g", "xq_norm_g")
    vectors = [p[name].astype(_F32) for name in vector_names]
    operands = [p["sinks"].astype(_F32), x, km, vm, cos, sin] + vectors + [
        w_in, w_gate, _head_mean_matrix(LANES), w_out]
    layer_block = lambda a: pl.BlockSpec((1,) + a.shape[1:], lambda b, s: (layer,) + (0,) * (a.ndim - 1))
    in_specs = [
        pl.BlockSpec(memory_space=pltpu.SMEM),
        pl.BlockSpec((1, tile, D_MODEL), lambda b, s: (b, s, 0)),
        pl.BlockSpec((1, MEM_LEN, XATTN_WIDTH), lambda b, s: (b, 0, 0)),
        pl.BlockSpec((1, MEM_LEN, XATTN_WIDTH), lambda b, s: (b, 0, 0)),
        _const_spec(cos.shape), _const_spec(sin.shape),
    ] + [layer_block(v) for v in vectors] + [_const_spec(a.shape) for a in operands[-4:]]

    return pl.pallas_call(
        functools.partial(_layer_kernel, tile=tile, layer=layer),
        out_shape=jax.ShapeDtypeStruct(x.shape, x.dtype),
        grid=(B, n_tiles),
        in_specs=in_specs,
        out_specs=pl.BlockSpec((1, tile, D_MODEL), lambda b, s: (b, s, 0)),
        scratch_shapes=[
            pltpu.VMEM((tile, _P_END), _F32),
            pltpu.VMEM((LRU_SLABS, SUBLANES * SEG_PITCH, LANES), _F32),
            pltpu.VMEM((LRU_SLABS, SUBLANES * SEG_PITCH, LANES), _F32),
            pltpu.VMEM((1, LRU_WIDTH), _F32),
            pltpu.VMEM(((CONV_WIDTH - 1) * SUBLANES, LRU_WIDTH), _F32),
            pltpu.VMEM((SWA_KV_HEADS, tile + BLOCK, LANES), _BF16),
            pltpu.VMEM((SWA_KV_HEADS, tile + BLOCK, LANES), _BF16),
            pltpu.VMEM((tile, D_MIX), _BF16),
        ],
        compiler_params=pltpu.CompilerParams(
            dimension_semantics=("arbitrary", "arbitrary"), vmem_limit_bytes=VMEM_LIMIT_BYTES),
        name="hymba_layer",
    )(*operands)


def _mem_kv(mem, p, layer):
    B = mem.shape[0]
    vectors = [p["mem_norm_g"].astype(_F32), p["xk_norm_g"].astype(_F32)]
    operands = [mem, vectors[0], p["w_mem_kv"][layer].astype(_BF16), vectors[1], _head_mean_matrix(XATTN_WIDTH)]
    layer_block = lambda a: pl.BlockSpec((1,) + a.shape[1:], lambda b: (layer,) + (0,) * (a.ndim - 1))
    kv_shape = jax.ShapeDtypeStruct((B, MEM_LEN, XATTN_WIDTH), _BF16)
    kv_spec = pl.BlockSpec((1, MEM_LEN, XATTN_WIDTH), lambda b: (b, 0, 0))
    return pl.pallas_call(
        _mem_kv_kernel,
        out_shape=(kv_shape, kv_shape),
        grid=(B,),
        in_specs=[pl.BlockSpec((1, MEM_LEN, D_MODEL), lambda b: (b, 0, 0)), layer_block(vectors[0]),
                  _const_spec(operands[2].shape), layer_block(vectors[1]), _const_spec(operands[4].shape)],
        out_specs=(kv_spec, kv_spec),
        compiler_params=pltpu.CompilerParams(dimension_semantics=("arbitrary",)),
        name="hymba_mem_kv",
    )(*operands)


def kernel(x, mem, norm_g, mem_norm_g, w_in, conv_w, conv_b, w_rg, b_rg, w_ig, b_ig, lru_lambda, q_norm_g,
           k_norm_g, sinks, w_mem_kv, xq_norm_g, xk_norm_g, out_norm_g, w_out):
    depth = w_in.shape[0]
    tile = TILE
    assert x.shape[1] % tile == 0 and tile % BLOCK == 0 and WINDOW == BLOCK
    params = dict(
        norm_g=norm_g, w_in=w_in, conv_w=conv_w, conv_b=conv_b, w_rg=w_rg, b_rg=b_rg, w_ig=w_ig, b_ig=b_ig,
        lru_lambda=lru_lambda, q_norm_g=q_norm_g, k_norm_g=k_norm_g, sinks=sinks, xq_norm_g=xq_norm_g,
        out_norm_g=out_norm_g, w_out=w_out, mem_norm_g=mem_norm_g, w_mem_kv=w_mem_kv, xk_norm_g=xk_norm_g)
    h = x
    for l in range(depth):
        km, vm = _mem_kv(mem, params, l)
        h = _layer(h, km, vm, params, l, tile=tile)
    return h
```

```python
import functools
import math

import jax
import jax.numpy as jnp
import numpy as np
from jax import lax
from jax.experimental import pallas as pl
from jax.experimental.pallas import tpu as pltpu

D_MODEL = 1024
MEM_LEN = 256
HEAD_DIM = 64
LRU_WIDTH = 512
LRU_BLOCKS = 8
LRU_BLOCK = LRU_WIDTH // LRU_BLOCKS
CONV_WIDTH = 4
LRU_C = 8.0
SWA_Q_HEADS = 4
SWA_KV_HEADS = 2
SWA_WIDTH = SWA_Q_HEADS * HEAD_DIM
SWA_KV_WIDTH = SWA_KV_HEADS * HEAD_DIM
WINDOW = 128
BLOCK = 128
XATTN_HEADS = 4
XATTN_WIDTH = XATTN_HEADS * HEAD_DIM
D_MIX = LRU_WIDTH + SWA_WIDTH + XATTN_WIDTH
ROPE_THETA = 500000.0
ROPE_DIM = HEAD_DIM // 4
EPS = 1e-6
NEG_INF = -1e30
LOG2_E = math.log2(math.e)

LANES = 128
SUBLANES = 8
MXU_DIM = 256
VMEM_LIMIT_BYTES = 56 * 1024 * 1024

TILE = 1024
SEG_LEN = TILE // SUBLANES
SEG_PITCH = SEG_LEN + SUBLANES
LRU_SLABS = LRU_WIDTH // LANES
SQRT_CLAMP = float(np.finfo(np.float32).tiny)

_P_GATE_A = 0
_P_Q = _P_GATE_A + LRU_WIDTH
_P_K = _P_Q + SWA_WIDTH
_P_V = _P_K + SWA_KV_WIDTH
_P_GATE_B = _P_V + SWA_KV_WIDTH
_P_XQ = _P_GATE_B + SWA_WIDTH
_P_GATE_C = _P_XQ + XATTN_WIDTH
_P_END = _P_GATE_C + XATTN_WIDTH

_HEAD_SLOTS = ((0, 0), (0, 1), (1, 0), (1, 1))

_BF16 = jnp.bfloat16
_F32 = jnp.float32


def _dot(a, b):
    return jnp.dot(a, b, preferred_element_type=_F32)


def _dot_nt(a, b):
    return lax.dot_general(a, b, (((1,), (1,)), ((), ())), preferred_element_type=_F32)


def _sigmoid_of_half(hz):
    return 0.5 * jnp.tanh(hz) + 0.5


def _silu_of_half(hz):
    return hz + hz * jnp.tanh(hz)


def _sqrt_nonneg(v):
    return v * lax.rsqrt(jnp.maximum(v, SQRT_CLAMP))


def _row_rms_scale(v):
    return lax.rsqrt(jnp.mean(v * v, axis=-1, keepdims=True) + EPS)


def _head_rms_scale(v, head_mean_ref):
    sq = (v * v).astype(_BF16)
    return lax.rsqrt(_dot(sq, head_mean_ref[...]) + EPS)


def _rope(v, cos, sin):
    half = ROPE_DIM // 2
    lane = lax.broadcasted_iota(jnp.int32, v.shape, 1) % HEAD_DIM
    partner = jnp.where(lane < half, pltpu.roll(v, LANES - half, axis=1), pltpu.roll(v, half, axis=1))
    return v * cos + partner * sin


def _layer_kernel(
    sinks_ref, x_ref, mem_ref, cos_ref, sin_ref, mem_g_ref, xkg_ref, norm_g_ref, out_g_ref, conv_w3_ref, conv_b_ref,
    b_rg_ref, b_ig_ref, lam_ref, qg_ref, kg_ref, xqg_ref, w_mem_ref, w_in_ref, w_gate_ref, head_mean_ref, w_out_ref,
    o_ref,
    proj_scr, conv_scr, h_scr, carry_scr, tail_scr, k_scr, v_scr, y_scr, km_scr, vm_scr,
    *, tile, layer,
):
    s = pl.program_id(1)
    n_blocks = tile // BLOCK
    conv_w_ref = conv_w3_ref.at[0]
    score_scale = LOG2_E / math.sqrt(HEAD_DIM)
    two_heads = lambda g_ref: jnp.concatenate([g_ref[...]] * (LANES // HEAD_DIM), axis=1)
    q_gain = two_heads(qg_ref) * score_scale
    k_gain = two_heads(kg_ref)
    xq_gain = two_heads(xqg_ref) * score_scale

    @pl.when(s == 0)
    def _reset_sequence_state():
        tail_scr[...] = jnp.zeros_like(tail_scr)
        carry_scr[...] = jnp.zeros_like(carry_scr)
        k_scr[:, tile:tile + BLOCK, :] = jnp.zeros((SWA_KV_HEADS, BLOCK, LANES), _BF16)
        v_scr[:, tile:tile + BLOCK, :] = jnp.zeros((SWA_KV_HEADS, BLOCK, LANES), _BF16)

    @pl.when(s == 0)
    def _project_memory():
        m = mem_ref[0]
        mn = (m * _row_rms_scale(m) * mem_g_ref[...]).astype(_BF16)
        kv = _dot(mn, w_mem_ref[...])
        xk_gain = two_heads(xkg_ref)
        for g in range(XATTN_WIDTH // LANES):
            k = kv[:, g * LANES:(g + 1) * LANES]
            km_scr[:, g * LANES:(g + 1) * LANES] = (k * _head_rms_scale(k, head_mean_ref) * xk_gain).astype(_BF16)
        vm_scr[...] = kv[:, XATTN_WIDTH:].astype(_BF16)

    k_scr[:, 0:BLOCK, :] = k_scr[:, tile:tile + BLOCK, :]
    v_scr[:, 0:BLOCK, :] = v_scr[:, tile:tile + BLOCK, :]

    x = x_ref[0]
    xn = (x * _row_rms_scale(x) * norm_g_ref[...]).astype(_BF16)
    lru_x = _dot(xn, w_in_ref[:, 0:LRU_WIDTH])
    for seg in range(SUBLANES):
        for slab in range(LRU_SLABS):
            conv_scr[slab, seg * SEG_PITCH:seg * SEG_PITCH + SEG_LEN, :] = (
                lru_x[seg * SEG_LEN:(seg + 1) * SEG_LEN, slab * LANES:(slab + 1) * LANES])
    proj_scr[...] = _dot(xn, w_in_ref[:, LRU_WIDTH:])

    seg_id = lax.broadcasted_iota(jnp.int32, (SUBLANES, LRU_WIDTH), 0)

    def strided(ref, g):
        return jnp.concatenate(
            [ref[slab, pl.ds(g, SUBLANES, stride=SEG_PITCH), :] for slab in range(LRU_SLABS)], axis=-1)

    def shift_segments(cur, prev_tile):
        return jnp.where(seg_id == 0, pltpu.roll(prev_tile, 1, axis=0), pltpu.roll(cur, 1, axis=0))

    u = [strided(conv_scr, g) for g in range(SEG_LEN)]
    wrapped = {}
    for k in range(1, CONV_WIDTH):
        wrapped[SEG_LEN - k] = shift_segments(u[SEG_LEN - k], tail_scr[(k - 1) * SUBLANES:k * SUBLANES, :])
    for k in range(1, CONV_WIDTH):
        tail_scr[(k - 1) * SUBLANES:k * SUBLANES, :] = u[SEG_LEN - k]
    xc_parts = []
    for g in range(SEG_LEN):
        xc = conv_b_ref[...] + conv_w_ref[CONV_WIDTH - 1:CONV_WIDTH, :] * u[g]
        for k in range(1, CONV_WIDTH):
            past = u[g - k] if g >= k else wrapped[SEG_LEN + g - k]
            xc = xc + conv_w_ref[CONV_WIDTH - 1 - k:CONV_WIDTH - k, :] * past
        xc_parts.append(xc)
    xc = jnp.concatenate(xc_parts, axis=0)

    lam = -lam_ref[...]
    softplus = jnp.maximum(lam, 0.0) + jnp.log1p(jnp.exp(-jnp.abs(lam)))
    neg_c_softplus = -LRU_C * softplus
    a_parts, b_parts = [], []
    for j in range(LRU_WIDTH // MXU_DIM):
        cols = slice(j * MXU_DIM, (j + 1) * MXU_DIM)
        pre = _dot(xc[:, cols].astype(_BF16), w_gate_ref[j])
        rg = _sigmoid_of_half(pre[:, :MXU_DIM] + 0.5 * b_rg_ref[:, cols])
        ig = _sigmoid_of_half(pre[:, MXU_DIM:] + 0.5 * b_ig_ref[:, cols])
        log_a = rg * neg_c_softplus[:, cols]
        a = jnp.exp(log_a)
        a_parts.append(a)
        b_parts.append(_sqrt_nonneg(jnp.tanh(-log_a) * (1.0 + a * a)) * (ig * xc[:, cols]))
    a_all = jnp.concatenate(a_parts, axis=-1)
    b_all = jnp.concatenate(b_parts, axis=-1)

    local, decay = [], []
    for g in range(SEG_LEN):
        a_g = a_all[g * SUBLANES:(g + 1) * SUBLANES, :]
        b_g = b_all[g * SUBLANES:(g + 1) * SUBLANES, :]
        local.append(b_g if g == 0 else a_g * local[-1] + b_g)
        decay.append(a_g if g == 0 else a_g * decay[-1])
    seg_a, seg_b = decay[-1], local[-1]
    d = 1
    while d < SUBLANES:
        keep = seg_id >= d
        a_prev = jnp.where(keep, pltpu.roll(seg_a, d, axis=0), 1.0)
        b_prev = jnp.where(keep, pltpu.roll(seg_b, d, axis=0), 0.0)
        seg_b = seg_b + seg_a * b_prev
        seg_a = seg_a * a_prev
        d *= 2
    carry = carry_scr[...]
    seg_end = seg_a * carry + seg_b
    seg_in = jnp.where(seg_id == 0, carry, pltpu.roll(seg_end, 1, axis=0))
    carry_scr[...] = seg_end[SUBLANES - 1:SUBLANES, :]
    for g in range(SEG_LEN):
        h_g = local[g] + decay[g] * seg_in
        for slab in range(LRU_SLABS):
            h_scr[slab, pl.ds(g, SUBLANES, stride=SEG_PITCH), :] = h_g[:, slab * LANES:(slab + 1) * LANES]

    h = jnp.concatenate([
        jnp.concatenate([h_scr[slab, seg * SEG_PITCH:seg * SEG_PITCH + SEG_LEN, :] for slab in range(LRU_SLABS)],
                        axis=-1)
        for seg in range(SUBLANES)], axis=0)
    gate_a = proj_scr[:, _P_GATE_A:_P_GATE_A + LRU_WIDTH]
    y_a = h * _row_rms_scale(h) * out_g_ref[:, 0:LRU_WIDTH] * _silu_of_half(gate_a)
    y_scr[:, 0:LRU_WIDTH] = y_a.astype(_BF16)

    pos = pl.ds(pl.multiple_of(s * tile, tile), tile)
    cos = cos_ref[pos, :]
    sin = sin_ref[pos, :]
    q_groups = []
    for g in range(SWA_WIDTH // LANES):
        cols = slice(_P_Q + g * LANES, _P_Q + (g + 1) * LANES)
        qraw = proj_scr[:, cols]
        qn = qraw * _head_rms_scale(qraw, head_mean_ref) * q_gain
        q_groups.append(_rope(qn, cos, sin))
    kraw = proj_scr[:, _P_K:_P_K + SWA_KV_WIDTH]
    kn = kraw * _head_rms_scale(kraw, head_mean_ref) * k_gain
    lane_t = lax.broadcasted_iota(jnp.int32, (tile, LANES), 1)
    for name_scr, kv in ((k_scr, _rope(kn, cos, sin)), (v_scr, proj_scr[:, _P_V:_P_V + SWA_KV_WIDTH])):
        swapped = pltpu.roll(kv, HEAD_DIM, axis=1)
        name_scr[0, BLOCK:BLOCK + tile, :] = jnp.where(lane_t < HEAD_DIM, kv, swapped).astype(_BF16)
        name_scr[1, BLOCK:BLOCK + tile, :] = jnp.where(lane_t < HEAD_DIM, swapped, kv).astype(_BF16)

    xq_groups = []
    for g in range(XATTN_WIDTH // LANES):
        cols = slice(_P_XQ + g * LANES, _P_XQ + (g + 1) * LANES)
        xraw = proj_scr[:, cols]
        xq_groups.append(xraw * _head_rms_scale(xraw, head_mean_ref) * xq_gain)

    lane = lax.broadcasted_iota(jnp.int32, (BLOCK, LANES), 1)
    low_half = lane < HEAD_DIM
    half_masks = (low_half, lane >= HEAD_DIM)
    qi = lax.broadcasted_iota(jnp.int32, (BLOCK, BLOCK), 0)
    kj = lax.broadcasted_iota(jnp.int32, (BLOCK, BLOCK), 1)
    own_block = kj <= qi

    for blk in range(n_blocks):
        rows = slice(blk * BLOCK, (blk + 1) * BLOCK)
        o_swa = [None, None]
        for head, (g, hi) in enumerate(_HEAD_SLOTS):
            kv_head = head // (SWA_Q_HEADS // SWA_KV_HEADS)
            kk = k_scr[kv_head, blk * BLOCK:(blk + 2) * BLOCK, :]
            vv = v_scr[kv_head, blk * BLOCK:(blk + 2) * BLOCK, :]
            qm = jnp.where(half_masks[hi], q_groups[g][rows, :], 0.0).astype(_BF16)
            sc_both = _dot_nt(qm, kk)
            sc_prev = sc_both[:, :BLOCK]
            if blk == 0:
                sc_prev = jnp.where(s > 0, sc_prev, NEG_INF)
            sc = jnp.where(own_block, sc_both[:, BLOCK:], sc_prev)
            sink = sinks_ref[layer, head] * LOG2_E
            m = jnp.maximum(jnp.max(sc, axis=-1, keepdims=True), sink)
            p = jnp.exp2(sc - m)
            den = jnp.sum(p, axis=-1, keepdims=True) + jnp.exp2(sink - m)
            p_both = jnp.concatenate([jnp.where(own_block, 0.0, p), jnp.where(own_block, p, 0.0)], axis=-1)
            o = _dot(p_both.astype(_BF16), vv) * (1.0 / den)
            o_swa[g] = o if o_swa[g] is None else jnp.where(low_half, o_swa[g], o)
        y_b = jnp.concatenate(o_swa, axis=-1)
        gate_b = proj_scr[rows, _P_GATE_B:_P_GATE_B + SWA_WIDTH]
        y_b = y_b * _row_rms_scale(y_b) * out_g_ref[:, LRU_WIDTH:LRU_WIDTH + SWA_WIDTH] * _silu_of_half(gate_b)
        y_scr[rows, LRU_WIDTH:LRU_WIDTH + SWA_WIDTH] = y_b.astype(_BF16)

        o_mem = [None, None]
        for head, (g, hi) in enumerate(_HEAD_SLOTS):
            qm = jnp.where(half_masks[hi], xq_groups[g][rows, :], 0.0).astype(_BF16)
            sc = _dot_nt(qm, km_scr[:, g * LANES:(g + 1) * LANES])
            m = jnp.max(sc, axis=-1, keepdims=True)
            p = jnp.exp2(sc - m)
            den = jnp.sum(p, axis=-1, keepdims=True)
            o = _dot(p.astype(_BF16), vm_scr[:, g * LANES:(g + 1) * LANES]) * (1.0 / den)
            o_mem[g] = o if o_mem[g] is None else jnp.where(low_half, o_mem[g], o)
        y_c = jnp.concatenate(o_mem, axis=-1)
        gate_c = proj_scr[rows, _P_GATE_C:_P_GATE_C + XATTN_WIDTH]
        y_c = y_c * _row_rms_scale(y_c) * out_g_ref[:, LRU_WIDTH + SWA_WIDTH:] * _silu_of_half(gate_c)
        y_scr[rows, LRU_WIDTH + SWA_WIDTH:] = y_c.astype(_BF16)

    o_ref[0] = x + _dot(y_scr[...], w_out_ref[...])


def _rope_tables(seq):
    pos = np.arange(seq, dtype=np.float32)
    inv_freq = np.float32(ROPE_THETA) ** (-(np.arange(0, ROPE_DIM, 2, dtype=np.float32) / np.float32(ROPE_DIM)))
    ang = (pos[:, None] * inv_freq[None, :].astype(np.float32)).astype(np.float32)
    cos, sin = np.cos(ang).astype(np.float32), np.sin(ang).astype(np.float32)
    ones = np.ones((seq, HEAD_DIM - ROPE_DIM), np.float32)
    cos_head = np.concatenate([cos, cos, ones], axis=-1)
    sin_head = np.concatenate([-sin, sin, 0.0 * ones], axis=-1)
    reps = LANES // HEAD_DIM
    return jnp.asarray(np.tile(cos_head, (1, reps))), jnp.asarray(np.tile(sin_head, (1, reps)))


def _head_mean_matrix(width):
    block = np.full((HEAD_DIM, HEAD_DIM), 1.0 / HEAD_DIM, np.float32)
    return jnp.asarray(np.kron(np.eye(width // HEAD_DIM, dtype=np.float32), block), dtype=_BF16)


def _const_spec(shape):
    return pl.BlockSpec(shape, lambda *_: (0,) * len(shape))


def _gate_half_columns():
    scale = np.ones((LRU_WIDTH + _P_END,), np.float32)
    for start, width in ((_P_GATE_A, LRU_WIDTH), (_P_GATE_B, SWA_WIDTH), (_P_GATE_C, XATTN_WIDTH)):
        scale[LRU_WIDTH + start:LRU_WIDTH + start + width] = 0.5
    return scale


def _layer(x, mem, p, layer, *, tile):
    B, S, _ = x.shape
    n_tiles = S // tile
    w_in = (p["w_in"][layer] * _gate_half_columns()).astype(_BF16)
    w_out = p["w_out"][layer].astype(_BF16)

    def block_diag(w):
        eye = jnp.eye(LRU_BLOCKS, dtype=w.dtype)
        return jnp.einsum("nde,nm->ndme", w, eye).reshape(LRU_WIDTH, LRU_WIDTH)

    w_r, w_i = block_diag(p["w_rg"][layer]), block_diag(p["w_ig"][layer])
    w_gate = (0.5 * jnp.stack([
        jnp.concatenate([w_r[c:c + MXU_DIM, c:c + MXU_DIM], w_i[c:c + MXU_DIM, c:c + MXU_DIM]], axis=-1)
        for c in range(0, LRU_WIDTH, MXU_DIM)])).astype(_BF16)

    cos, sin = _rope_tables(S)
    vector_names = ("mem_norm_g", "xk_norm_g", "norm_g", "out_norm_g", "conv_w", "conv_b", "b_rg", "b_ig",
                    "lru_lambda", "q_norm_g", "k_norm_g", "xq_norm_g")
    vectors = [p[name].astype(_F32) for name in vector_names]
    w_mem = p["w_mem_kv"][layer].astype(_BF16)
    operands = [p["sinks"].astype(_F32), x, mem, cos, sin] + vectors + [
        w_mem, w_in, w_gate, _head_mean_matrix(LANES), w_out]
    layer_block = lambda a: pl.BlockSpec((1,) + a.shape[1:], lambda b, s: (layer,) + (0,) * (a.ndim - 1))
    in_specs = [
        pl.BlockSpec(memory_space=pltpu.SMEM),
        pl.BlockSpec((1, tile, D_MODEL), lambda b, s: (b, s, 0)),
        pl.BlockSpec((1, MEM_LEN, D_MODEL), lambda b, s: (b, 0, 0)),
        _const_spec(cos.shape), _const_spec(sin.shape),
    ] + [layer_block(v) for v in vectors] + [_const_spec(a.shape) for a in operands[-5:]]

    return pl.pallas_call(
        functools.partial(_layer_kernel, tile=tile, layer=layer),
        out_shape=jax.ShapeDtypeStruct(x.shape, x.dtype),
        grid=(B, n_tiles),
        in_specs=in_specs,
        out_specs=pl.BlockSpec((1, tile, D_MODEL), lambda b, s: (b, s, 0)),
        scratch_shapes=[
            pltpu.VMEM((tile, _P_END), _F32),
            pltpu.VMEM((LRU_SLABS, SUBLANES * SEG_PITCH, LANES), _F32),
            pltpu.VMEM((LRU_SLABS, SUBLANES * SEG_PITCH, LANES), _F32),
            pltpu.VMEM((1, LRU_WIDTH), _F32),
            pltpu.VMEM(((CONV_WIDTH - 1) * SUBLANES, LRU_WIDTH), _F32),
            pltpu.VMEM((SWA_KV_HEADS, tile + BLOCK, LANES), _BF16),
            pltpu.VMEM((SWA_KV_HEADS, tile + BLOCK, LANES), _BF16),
            pltpu.VMEM((tile, D_MIX), _BF16),
            pltpu.VMEM((MEM_LEN, XATTN_WIDTH), _BF16),
            pltpu.VMEM((MEM_LEN, XATTN_WIDTH), _BF16),
        ],
        compiler_params=pltpu.CompilerParams(
            dimension_semantics=("arbitrary", "arbitrary"), vmem_limit_bytes=VMEM_LIMIT_BYTES),
        name="hymba_layer",
    )(*operands)


def kernel(x, mem, norm_g, mem_norm_g, w_in, conv_w, conv_b, w_rg, b_rg, w_ig, b_ig, lru_lambda, q_norm_g,
           k_norm_g, sinks, w_mem_kv, xq_norm_g, xk_norm_g, out_norm_g, w_out):
    depth = w_in.shape[0]
    tile = TILE
    assert x.shape[1] % tile == 0 and tile % BLOCK == 0 and WINDOW == BLOCK
    params = dict(
        norm_g=norm_g, w_in=w_in, conv_w=conv_w, conv_b=conv_b, w_rg=w_rg, b_rg=b_rg, w_ig=w_ig, b_ig=b_ig,
        lru_lambda=lru_lambda, q_norm_g=q_norm_g, k_norm_g=k_norm_g, sinks=sinks, xq_norm_g=xq_norm_g,
        out_norm_g=out_norm_g, w_out=w_out, mem_norm_g=mem_norm_g, w_mem_kv=w_mem_kv, xk_norm_g=xk_norm_g)
    h = x
    for l in range(depth):
        h = _layer(h, mem, params, l, tile=tile)
    return h
```

```python
import functools
import math

import jax
import jax.numpy as jnp
import numpy as np
from jax import lax
from jax.experimental import pallas as pl
from jax.experimental.pallas import tpu as pltpu

D_MODEL = 1024
MEM_LEN = 256
HEAD_DIM = 64
LRU_WIDTH = 512
LRU_BLOCKS = 8
LRU_BLOCK = LRU_WIDTH // LRU_BLOCKS
CONV_WIDTH = 4
LRU_C = 8.0
SWA_Q_HEADS = 4
SWA_KV_HEADS = 2
SWA_WIDTH = SWA_Q_HEADS * HEAD_DIM
SWA_KV_WIDTH = SWA_KV_HEADS * HEAD_DIM
WINDOW = 128
BLOCK = 128
XATTN_HEADS = 4
XATTN_WIDTH = XATTN_HEADS * HEAD_DIM
D_MIX = LRU_WIDTH + SWA_WIDTH + XATTN_WIDTH
ROPE_THETA = 500000.0
ROPE_DIM = HEAD_DIM // 4
EPS = 1e-6
NEG_INF = -1e30
LOG2_E = math.log2(math.e)

LANES = 128
SUBLANES = 8
MXU_DIM = 256
VMEM_LIMIT_BYTES = 56 * 1024 * 1024

TILE = 1024
SEG_LEN = TILE // SUBLANES
SEG_PITCH = SEG_LEN + SUBLANES
LRU_SLABS = LRU_WIDTH // LANES
SQRT_CLAMP = float(np.finfo(np.float32).tiny)

_P_GATE_A = 0
_P_Q = _P_GATE_A + LRU_WIDTH
_P_K = _P_Q + SWA_WIDTH
_P_V = _P_K + SWA_KV_WIDTH
_P_GATE_B = _P_V + SWA_KV_WIDTH
_P_XQ = _P_GATE_B + SWA_WIDTH
_P_GATE_C = _P_XQ + XATTN_WIDTH
_P_END = _P_GATE_C + XATTN_WIDTH

_HEAD_SLOTS = ((0, 0), (0, 1), (1, 0), (1, 1))

_BF16 = jnp.bfloat16
_F32 = jnp.float32


def _dot(a, b):
    return jnp.dot(a, b, preferred_element_type=_F32)


def _dot_nt(a, b):
    return lax.dot_general(a, b, (((1,), (1,)), ((), ())), preferred_element_type=_F32)


def _sigmoid_of_half(hz):
    return 0.5 * jnp.tanh(hz) + 0.5


def _silu_of_half(hz):
    return hz + hz * jnp.tanh(hz)


def _sqrt_nonneg(v):
    return v * lax.rsqrt(jnp.maximum(v, SQRT_CLAMP))


def _row_rms_scale(v):
    return lax.rsqrt(jnp.mean(v * v, axis=-1, keepdims=True) + EPS)


def _head_rms_scale(v, head_mean_ref):
    sq = (v * v).astype(_BF16)
    return lax.rsqrt(_dot(sq, head_mean_ref[...]) + EPS)


def _rope(v, cos, sin):
    half = ROPE_DIM // 2
    lane = lax.broadcasted_iota(jnp.int32, v.shape, 1) % HEAD_DIM
    partner = jnp.where(lane < half, pltpu.roll(v, LANES - half, axis=1), pltpu.roll(v, half, axis=1))
    return v * cos + partner * sin


def _layer_kernel(
    sinks_ref, x_ref, mem_ref, cos_ref, sin_ref, mem_g_ref, xkg_ref, norm_g_ref, conv_w3_ref, conv_b_ref,
    b_rg_ref, b_ig_ref, lam_ref, qg_ref, kg_ref, xqg_ref, w_mem_ref, w_in_ref, w_gate_ref, head_mean_ref, w_out_ref,
    o_ref,
    proj_scr, conv_scr, h_scr, carry_scr, tail_scr, k_scr, v_scr, y_scr, km_scr, vm_scr,
    *, tile, layer,
):
    s = pl.program_id(1)
    n_blocks = tile // BLOCK
    conv_w_ref = conv_w3_ref.at[0]
    score_scale = LOG2_E / math.sqrt(HEAD_DIM)
    two_heads = lambda g_ref: jnp.concatenate([g_ref[...]] * (LANES // HEAD_DIM), axis=1)
    q_gain = two_heads(qg_ref) * score_scale
    k_gain = two_heads(kg_ref)
    xq_gain = two_heads(xqg_ref) * score_scale

    @pl.when(s == 0)
    def _reset_sequence_state():
        tail_scr[...] = jnp.zeros_like(tail_scr)
        carry_scr[...] = jnp.zeros_like(carry_scr)
        k_scr[:, tile:tile + BLOCK, :] = jnp.zeros((SWA_KV_HEADS, BLOCK, LANES), _BF16)
        v_scr[:, tile:tile + BLOCK, :] = jnp.zeros((SWA_KV_HEADS, BLOCK, LANES), _BF16)

    @pl.when(s == 0)
    def _project_memory():
        m = mem_ref[0]
        mn = (m * _row_rms_scale(m) * mem_g_ref[...]).astype(_BF16)
        kv = _dot(mn, w_mem_ref[...])
        xk_gain = two_heads(xkg_ref)
        for g in range(XATTN_WIDTH // LANES):
            k = kv[:, g * LANES:(g + 1) * LANES]
            km_scr[:, g * LANES:(g + 1) * LANES] = (k * _head_rms_scale(k, head_mean_ref) * xk_gain).astype(_BF16)
        vm_scr[...] = kv[:, XATTN_WIDTH:].astype(_BF16)

    k_scr[:, 0:BLOCK, :] = k_scr[:, tile:tile + BLOCK, :]
    v_scr[:, 0:BLOCK, :] = v_scr[:, tile:tile + BLOCK, :]

    x = x_ref[0]
    xn = (x * _row_rms_scale(x) * norm_g_ref[...]).astype(_BF16)
    lru_x = _dot(xn, w_in_ref[:, 0:LRU_WIDTH])
    for seg in range(SUBLANES):
        for slab in range(LRU_SLABS):
            conv_scr[slab, seg * SEG_PITCH:seg * SEG_PITCH + SEG_LEN, :] = (
                lru_x[seg * SEG_LEN:(seg + 1) * SEG_LEN, slab * LANES:(slab + 1) * LANES])
    proj_scr[...] = _dot(xn, w_in_ref[:, LRU_WIDTH:])

    seg_id = lax.broadcasted_iota(jnp.int32, (SUBLANES, LRU_WIDTH), 0)

    def strided(ref, g):
        return jnp.concatenate(
            [ref[slab, pl.ds(g, SUBLANES, stride=SEG_PITCH), :] for slab in range(LRU_SLABS)], axis=-1)

    def shift_segments(cur, prev_tile):
        return jnp.where(seg_id == 0, pltpu.roll(prev_tile, 1, axis=0), pltpu.roll(cur, 1, axis=0))

    u = [strided(conv_scr, g) for g in range(SEG_LEN)]
    wrapped = {}
    for k in range(1, CONV_WIDTH):
        wrapped[SEG_LEN - k] = shift_segments(u[SEG_LEN - k], tail_scr[(k - 1) * SUBLANES:k * SUBLANES, :])
    for k in range(1, CONV_WIDTH):
        tail_scr[(k - 1) * SUBLANES:k * SUBLANES, :] = u[SEG_LEN - k]
    xc_parts = []
    for g in range(SEG_LEN):
        xc = conv_b_ref[...] + conv_w_ref[CONV_WIDTH - 1:CONV_WIDTH, :] * u[g]
        for k in range(1, CONV_WIDTH):
            past = u[g - k] if g >= k else wrapped[SEG_LEN + g - k]
            xc = xc + conv_w_ref[CONV_WIDTH - 1 - k:CONV_WIDTH - k, :] * past
        xc_parts.append(xc)
    xc = jnp.concatenate(xc_parts, axis=0)

    lam = -lam_ref[...]
    softplus = jnp.maximum(lam, 0.0) + jnp.log1p(jnp.exp(-jnp.abs(lam)))
    c_softplus = LRU_C * softplus
    neg_c_softplus_log2e = -LOG2_E * c_softplus
    a_parts, b_parts = [], []
    for j in range(LRU_WIDTH // MXU_DIM):
        cols = slice(j * MXU_DIM, (j + 1) * MXU_DIM)
        pre = _dot(xc[:, cols].astype(_BF16), w_gate_ref[j])
        rg = _sigmoid_of_half(pre[:, :MXU_DIM] + 0.5 * b_rg_ref[:, cols])
        ig = _sigmoid_of_half(pre[:, MXU_DIM:] + 0.5 * b_ig_ref[:, cols])
        a = jnp.exp2(rg * neg_c_softplus_log2e[:, cols])
        a_parts.append(a)
        b_parts.append(_sqrt_nonneg(jnp.tanh(rg * c_softplus[:, cols]) * (1.0 + a * a)) * (ig * xc[:, cols]))
    a_all = jnp.concatenate(a_parts, axis=-1)
    b_all = jnp.concatenate(b_parts, axis=-1)

    local, decay = [], []
    for g in range(SEG_LEN):
        a_g = a_all[g * SUBLANES:(g + 1) * SUBLANES, :]
        b_g = b_all[g * SUBLANES:(g + 1) * SUBLANES, :]
        local.append(b_g if g == 0 else a_g * local[-1] + b_g)
        decay.append(a_g if g == 0 else a_g * decay[-1])
    seg_a, seg_b = decay[-1], local[-1]
    d = 1
    while d < SUBLANES:
        keep = seg_id >= d
        a_prev = jnp.where(keep, pltpu.roll(seg_a, d, axis=0), 1.0)
        b_prev = jnp.where(keep, pltpu.roll(seg_b, d, axis=0), 0.0)
        seg_b = seg_b + seg_a * b_prev
        seg_a = seg_a * a_prev
        d *= 2
    carry = carry_scr[...]
    seg_end = seg_a * carry + seg_b
    seg_in = jnp.where(seg_id == 0, carry, pltpu.roll(seg_end, 1, axis=0))
    carry_scr[...] = seg_end[SUBLANES - 1:SUBLANES, :]
    for g in range(SEG_LEN):
        h_g = local[g] + decay[g] * seg_in
        for slab in range(LRU_SLABS):
            h_scr[slab, pl.ds(g, SUBLANES, stride=SEG_PITCH), :] = h_g[:, slab * LANES:(slab + 1) * LANES]

    h = jnp.concatenate([
        jnp.concatenate([h_scr[slab, seg * SEG_PITCH:seg * SEG_PITCH + SEG_LEN, :] for slab in range(LRU_SLABS)],
                        axis=-1)
        for seg in range(SUBLANES)], axis=0)
    gate_a = proj_scr[:, _P_GATE_A:_P_GATE_A + LRU_WIDTH]
    y_a = h * _row_rms_scale(h) * _silu_of_half(gate_a)
    y_scr[:, 0:LRU_WIDTH] = y_a.astype(_BF16)

    pos = pl.ds(pl.multiple_of(s * tile, tile), tile)
    cos = cos_ref[pos, :]
    sin = sin_ref[pos, :]
    q_groups = []
    for g in range(SWA_WIDTH // LANES):
        cols = slice(_P_Q + g * LANES, _P_Q + (g + 1) * LANES)
        qraw = proj_scr[:, cols]
        qn = qraw * _head_rms_scale(qraw, head_mean_ref) * q_gain
        q_groups.append(_rope(qn, cos, sin))
    kraw = proj_scr[:, _P_K:_P_K + SWA_KV_WIDTH]
    kn = kraw * _head_rms_scale(kraw, head_mean_ref) * k_gain
    lane_t = lax.broadcasted_iota(jnp.int32, (tile, LANES), 1)
    for name_scr, kv in ((k_scr, _rope(kn, cos, sin)), (v_scr, proj_scr[:, _P_V:_P_V + SWA_KV_WIDTH])):
        swapped = pltpu.roll(kv, HEAD_DIM, axis=1)
        name_scr[0, BLOCK:BLOCK + tile, :] = jnp.where(lane_t < HEAD_DIM, kv, swapped).astype(_BF16)
        name_scr[1, BLOCK:BLOCK + tile, :] = jnp.where(lane_t < HEAD_DIM, swapped, kv).astype(_BF16)

    xq_groups = []
    for g in range(XATTN_WIDTH // LANES):
        cols = slice(_P_XQ + g * LANES, _P_XQ + (g + 1) * LANES)
        xraw = proj_scr[:, cols]
        xq_groups.append(xraw * _head_rms_scale(xraw, head_mean_ref) * xq_gain)

    lane = lax.broadcasted_iota(jnp.int32, (BLOCK, LANES), 1)
    low_half = lane < HEAD_DIM
    half_masks = (low_half, lane >= HEAD_DIM)
    qi = lax.broadcasted_iota(jnp.int32, (BLOCK, BLOCK), 0)
    kj = lax.broadcasted_iota(jnp.int32, (BLOCK, BLOCK), 1)
    own_block = kj <= qi

    for blk in range(n_blocks):
        rows = slice(blk * BLOCK, (blk + 1) * BLOCK)
        o_swa = [None, None]
        for head, (g, hi) in enumerate(_HEAD_SLOTS):
            kv_head = head // (SWA_Q_HEADS // SWA_KV_HEADS)
            kk = k_scr[kv_head, blk * BLOCK:(blk + 2) * BLOCK, :]
            vv = v_scr[kv_head, blk * BLOCK:(blk + 2) * BLOCK, :]
            qm = jnp.where(half_masks[hi], q_groups[g][rows, :], 0.0).astype(_BF16)
            sc_both = _dot_nt(qm, kk)
            sc_prev = sc_both[:, :BLOCK]
            if blk == 0:
                sc_prev = jnp.where(s > 0, sc_prev, NEG_INF)
            sc = jnp.where(own_block, sc_both[:, BLOCK:], sc_prev)
            sink = sinks_ref[layer, head] * LOG2_E
            m = jnp.maximum(jnp.max(sc, axis=-1, keepdims=True), sink)
            p = jnp.exp2(sc - m)
            den = jnp.sum(p, axis=-1, keepdims=True) + jnp.exp2(sink - m)
            p_both = jnp.concatenate([jnp.where(own_block, 0.0, p), jnp.where(own_block, p, 0.0)], axis=-1)
            o = _dot(p_both.astype(_BF16), vv) * (1.0 / den)
            o_swa[g] = o if o_swa[g] is None else jnp.where(low_half, o_swa[g], o)
        y_b = jnp.concatenate(o_swa, axis=-1)
        gate_b = proj_scr[rows, _P_GATE_B:_P_GATE_B + SWA_WIDTH]
        y_b = y_b * _row_rms_scale(y_b) * _silu_of_half(gate_b)
        y_scr[rows, LRU_WIDTH:LRU_WIDTH + SWA_WIDTH] = y_b.astype(_BF16)

        o_mem = [None, None]
        for head, (g, hi) in enumerate(_HEAD_SLOTS):
            qm = jnp.where(half_masks[hi], xq_groups[g][rows, :], 0.0).astype(_BF16)
            sc = _dot_nt(qm, km_scr[:, g * LANES:(g + 1) * LANES])
            m = jnp.max(sc, axis=-1, keepdims=True)
            p = jnp.exp2(sc - m)
            den = jnp.sum(p, axis=-1, keepdims=True)
            o = _dot(p.astype(_BF16), vm_scr[:, g * LANES:(g + 1) * LANES]) * (1.0 / den)
            o_mem[g] = o if o_mem[g] is None else jnp.where(low_half, o_mem[g], o)
        y_c = jnp.concatenate(o_mem, axis=-1)
        gate_c = proj_scr[rows, _P_GATE_C:_P_GATE_C + XATTN_WIDTH]
        y_c = y_c * _row_rms_scale(y_c) * _silu_of_half(gate_c)
        y_scr[rows, LRU_WIDTH + SWA_WIDTH:] = y_c.astype(_BF16)

    o_ref[0] = x + _dot(y_scr[...], w_out_ref[...])


def _rope_tables(seq):
    pos = np.arange(seq, dtype=np.float32)
    inv_freq = np.float32(ROPE_THETA) ** (-(np.arange(0, ROPE_DIM, 2, dtype=np.float32) / np.float32(ROPE_DIM)))
    ang = (pos[:, None] * inv_freq[None, :].astype(np.float32)).astype(np.float32)
    cos, sin = np.cos(ang).astype(np.float32), np.sin(ang).astype(np.float32)
    ones = np.ones((seq, HEAD_DIM - ROPE_DIM), np.float32)
    cos_head = np.concatenate([cos, cos, ones], axis=-1)
    sin_head = np.concatenate([-sin, sin, 0.0 * ones], axis=-1)
    reps = LANES // HEAD_DIM
    return jnp.asarray(np.tile(cos_head, (1, reps))), jnp.asarray(np.tile(sin_head, (1, reps)))


def _head_mean_matrix(width):
    block = np.full((HEAD_DIM, HEAD_DIM), 1.0 / HEAD_DIM, np.float32)
    return jnp.asarray(np.kron(np.eye(width // HEAD_DIM, dtype=np.float32), block), dtype=_BF16)


def _const_spec(shape):
    return pl.BlockSpec(shape, lambda *_: (0,) * len(shape))


def _gate_half_columns():
    scale = np.ones((LRU_WIDTH + _P_END,), np.float32)
    for start, width in ((_P_GATE_A, LRU_WIDTH), (_P_GATE_B, SWA_WIDTH), (_P_GATE_C, XATTN_WIDTH)):
        scale[LRU_WIDTH + start:LRU_WIDTH + start + width] = 0.5
    return scale


def _layer(x, mem, p, layer, *, tile):
    B, S, _ = x.shape
    n_tiles = S // tile
    w_in = (p["w_in"][layer] * _gate_half_columns()).astype(_BF16)
    w_out = (p["out_norm_g"][layer][:, None] * p["w_out"][layer]).astype(_BF16)

    def block_diag(w):
        eye = jnp.eye(LRU_BLOCKS, dtype=w.dtype)
        return jnp.einsum("nde,nm->ndme", w, eye).reshape(LRU_WIDTH, LRU_WIDTH)

    w_r, w_i = block_diag(p["w_rg"][layer]), block_diag(p["w_ig"][layer])
    w_gate = (0.5 * jnp.stack([
        jnp.concatenate([w_r[c:c + MXU_DIM, c:c + MXU_DIM], w_i[c:c + MXU_DIM, c:c + MXU_DIM]], axis=-1)
        for c in range(0, LRU_WIDTH, MXU_DIM)])).astype(_BF16)

    cos, sin = _rope_tables(S)
    vector_names = ("mem_norm_g", "xk_norm_g", "norm_g", "conv_w", "conv_b", "b_rg", "b_ig",
                    "lru_lambda", "q_norm_g", "k_norm_g", "xq_norm_g")
    vectors = [p[name].astype(_F32) for name in vector_names]
    w_mem = p["w_mem_kv"][layer].astype(_BF16)
    operands = [p["sinks"].astype(_F32), x, mem, cos, sin] + vectors + [
        w_mem, w_in, w_gate, _head_mean_matrix(LANES), w_out]
    layer_block = lambda a: pl.BlockSpec((1,) + a.shape[1:], lambda b, s: (layer,) + (0,) * (a.ndim - 1))
    in_specs = [
        pl.BlockSpec(memory_space=pltpu.SMEM),
        pl.BlockSpec((1, tile, D_MODEL), lambda b, s: (b, s, 0)),
        pl.BlockSpec((1, MEM_LEN, D_MODEL), lambda b, s: (b, 0, 0)),
        _const_spec(cos.shape), _const_spec(sin.shape),
    ] + [layer_block(v) for v in vectors] + [_const_spec(a.shape) for a in operands[-5:]]

    return pl.pallas_call(
        functools.partial(_layer_kernel, tile=tile, layer=layer),
        out_shape=jax.ShapeDtypeStruct(x.shape, x.dtype),
        grid=(B, n_tiles),
        in_specs=in_specs,
        out_specs=pl.BlockSpec((1, tile, D_MODEL), lambda b, s: (b, s, 0)),
        scratch_shapes=[
            pltpu.VMEM((tile, _P_END), _F32),
            pltpu.VMEM((LRU_SLABS, SUBLANES * SEG_PITCH, LANES), _F32),
            pltpu.VMEM((LRU_SLABS, SUBLANES * SEG_PITCH, LANES), _F32),
            pltpu.VMEM((1, LRU_WIDTH), _F32),
            pltpu.VMEM(((CONV_WIDTH - 1) * SUBLANES, LRU_WIDTH), _F32),
            pltpu.VMEM((SWA_KV_HEADS, tile + BLOCK, LANES), _BF16),
            pltpu.VMEM((SWA_KV_HEADS, tile + BLOCK, LANES), _BF16),
            pltpu.VMEM((tile, D_MIX), _BF16),
            pltpu.VMEM((MEM_LEN, XATTN_WIDTH), _BF16),
            pltpu.VMEM((MEM_LEN, XATTN_WIDTH), _BF16),
        ],
        compiler_params=pltpu.CompilerParams(
            dimension_semantics=("arbitrary", "arbitrary"), vmem_limit_bytes=VMEM_LIMIT_BYTES),
        name="hymba_layer",
    )(*operands)


def kernel(x, mem, norm_g, mem_norm_g, w_in, conv_w, conv_b, w_rg, b_rg, w_ig, b_ig, lru_lambda, q_norm_g,
           k_norm_g, sinks, w_mem_kv, xq_norm_g, xk_norm_g, out_norm_g, w_out):
    depth = w_in.shape[0]
    tile = TILE
    assert x.shape[1] % tile == 0 and tile % BLOCK == 0 and WINDOW == BLOCK
    params = dict(
        norm_g=norm_g, w_in=w_in, conv_w=conv_w, conv_b=conv_b, w_rg=w_rg, b_rg=b_rg, w_ig=w_ig, b_ig=b_ig,
        lru_lambda=lru_lambda, q_norm_g=q_norm_g, k_norm_g=k_norm_g, sinks=sinks, xq_norm_g=xq_norm_g,
        out_norm_g=out_norm_g, w_out=w_out, mem_norm_g=mem_norm_g, w_mem_kv=w_mem_kv, xk_norm_g=xk_norm_g)
    h = x
    for l in range(depth):
        h = _layer(h, mem, params, l, tile=tile)
    return h
```

```python
import functools
import math

import jax
import jax.numpy as jnp
import numpy as np
from jax import lax
from jax.experimental import pallas as pl
from jax.experimental.pallas import tpu as pltpu

D_MODEL = 1024
MEM_LEN = 256
HEAD_DIM = 64
LRU_WIDTH = 512
LRU_BLOCKS = 8
LRU_BLOCK = LRU_WIDTH // LRU_BLOCKS
CONV_WIDTH = 4
LRU_C = 8.0
SWA_Q_HEADS = 4
SWA_KV_HEADS = 2
SWA_WIDTH = SWA_Q_HEADS * HEAD_DIM
SWA_KV_WIDTH = SWA_KV_HEADS * HEAD_DIM
WINDOW = 128
BLOCK = 128
XATTN_HEADS = 4
XATTN_WIDTH = XATTN_HEADS * HEAD_DIM
D_MIX = LRU_WIDTH + SWA_WIDTH + XATTN_WIDTH
ROPE_THETA = 500000.0
ROPE_DIM = HEAD_DIM // 4
EPS = 1e-6
NEG_INF = -1e30
LOG2_E = math.log2(math.e)

LANES = 128
SUBLANES = 8
MXU_DIM = 256
VMEM_LIMIT_BYTES = 56 * 1024 * 1024

TILE = 1024
SEG_LEN = TILE // SUBLANES
SEG_PITCH = SEG_LEN + SUBLANES
LRU_SLABS = LRU_WIDTH // LANES
SQRT_CLAMP = float(np.finfo(np.float32).tiny)

_P_GATE_A = 0
_P_Q = _P_GATE_A + LRU_WIDTH
_P_K = _P_Q + SWA_WIDTH
_P_V = _P_K + SWA_KV_WIDTH
_P_GATE_B = _P_V + SWA_KV_WIDTH
_P_XQ = _P_GATE_B + SWA_WIDTH
_P_GATE_C = _P_XQ + XATTN_WIDTH
_P_END = _P_GATE_C + XATTN_WIDTH

_HEAD_SLOTS = ((0, 0), (0, 1), (1, 0), (1, 1))

_BF16 = jnp.bfloat16
_F32 = jnp.float32


def _dot(a, b):
    return jnp.dot(a, b, preferred_element_type=_F32)


def _dot_nt(a, b):
    return lax.dot_general(a, b, (((1,), (1,)), ((), ())), preferred_element_type=_F32)


def _sigmoid_of_half(hz):
    return 0.5 * jnp.tanh(hz) + 0.5


def _gated(v, hz):
    hz = hz.astype(_BF16)
    return v.astype(_BF16) * (hz + hz * jnp.tanh(hz))


def _sqrt_nonneg(v):
    return v * lax.rsqrt(jnp.maximum(v, SQRT_CLAMP))


def _row_rms_scale(v):
    return lax.rsqrt(jnp.mean(v * v, axis=-1, keepdims=True) + EPS)


def _head_rms_scale(v, head_mean_ref):
    sq = (v * v).astype(_BF16)
    return lax.rsqrt(_dot(sq, head_mean_ref[...]) + EPS)


def _rope(v, cos, sin):
    half = ROPE_DIM // 2
    lane = lax.broadcasted_iota(jnp.int32, v.shape, 1) % HEAD_DIM
    partner = jnp.where(lane < half, pltpu.roll(v, LANES - half, axis=1), pltpu.roll(v, half, axis=1))
    return v * cos + partner * sin


def _layer_kernel(
    sinks_ref, x_ref, mem_ref, cos_ref, sin_ref, mem_g_ref, xkg_ref, norm_g_ref, conv_w3_ref, conv_b_ref,
    b_rg_ref, b_ig_ref, lam_ref, qg_ref, kg_ref, xqg_ref, w_mem_ref, w_in_ref, w_gate_ref, head_mean_ref, w_out_ref,
    o_ref,
    proj_scr, conv_scr, h_scr, carry_scr, tail_scr, k_scr, v_scr, y_scr, km_scr, vm_scr,
    *, tile, layer,
):
    s = pl.program_id(1)
    n_blocks = tile // BLOCK
    conv_w_ref = conv_w3_ref.at[0]
    score_scale = LOG2_E / math.sqrt(HEAD_DIM)
    two_heads = lambda g_ref: jnp.concatenate([g_ref[...]] * (LANES // HEAD_DIM), axis=1)
    q_gain = two_heads(qg_ref) * score_scale
    k_gain = two_heads(kg_ref)
    xq_gain = two_heads(xqg_ref) * score_scale

    @pl.when(s == 0)
    def _reset_sequence_state():
        tail_scr[...] = jnp.zeros_like(tail_scr)
        carry_scr[...] = jnp.zeros_like(carry_scr)
        k_scr[:, tile:tile + BLOCK, :] = jnp.zeros((SWA_KV_HEADS, BLOCK, LANES), _BF16)
        v_scr[:, tile:tile + BLOCK, :] = jnp.zeros((SWA_KV_HEADS, BLOCK, LANES), _BF16)

    @pl.when(s == 0)
    def _project_memory():
        m = mem_ref[0]
        mn = (m * _row_rms_scale(m) * mem_g_ref[...]).astype(_BF16)
        kv = _dot(mn, w_mem_ref[...])
        xk_gain = two_heads(xkg_ref)
        for g in range(XATTN_WIDTH // LANES):
            k = kv[:, g * LANES:(g + 1) * LANES]
            km_scr[:, g * LANES:(g + 1) * LANES] = (k * _head_rms_scale(k, head_mean_ref) * xk_gain).astype(_BF16)
        vm_scr[...] = kv[:, XATTN_WIDTH:].astype(_BF16)

    k_scr[:, 0:BLOCK, :] = k_scr[:, tile:tile + BLOCK, :]
    v_scr[:, 0:BLOCK, :] = v_scr[:, tile:tile + BLOCK, :]

    x = x_ref[0]
    xn = (x * _row_rms_scale(x) * norm_g_ref[...]).astype(_BF16)
    lru_x = _dot(xn, w_in_ref[:, 0:LRU_WIDTH])
    for seg in range(SUBLANES):
        for slab in range(LRU_SLABS):
            conv_scr[slab, seg * SEG_PITCH:seg * SEG_PITCH + SEG_LEN, :] = (
                lru_x[seg * SEG_LEN:(seg + 1) * SEG_LEN, slab * LANES:(slab + 1) * LANES])
    proj_scr[...] = _dot(xn, w_in_ref[:, LRU_WIDTH:])

    seg_id = lax.broadcasted_iota(jnp.int32, (SUBLANES, LRU_WIDTH), 0)

    def strided(ref, g):
        return jnp.concatenate(
            [ref[slab, pl.ds(g, SUBLANES, stride=SEG_PITCH), :] for slab in range(LRU_SLABS)], axis=-1)

    def shift_segments(cur, prev_tile):
        return jnp.where(seg_id == 0, pltpu.roll(prev_tile, 1, axis=0), pltpu.roll(cur, 1, axis=0))

    u = [strided(conv_scr, g) for g in range(SEG_LEN)]
    wrapped = {}
    for k in range(1, CONV_WIDTH):
        wrapped[SEG_LEN - k] = shift_segments(u[SEG_LEN - k], tail_scr[(k - 1) * SUBLANES:k * SUBLANES, :])
    for k in range(1, CONV_WIDTH):
        tail_scr[(k - 1) * SUBLANES:k * SUBLANES, :] = u[SEG_LEN - k]
    xc_parts = []
    for g in range(SEG_LEN):
        xc = conv_b_ref[...] + conv_w_ref[CONV_WIDTH - 1:CONV_WIDTH, :] * u[g]
        for k in range(1, CONV_WIDTH):
            past = u[g - k] if g >= k else wrapped[SEG_LEN + g - k]
            xc = xc + conv_w_ref[CONV_WIDTH - 1 - k:CONV_WIDTH - k, :] * past
        xc_parts.append(xc)
    xc = jnp.concatenate(xc_parts, axis=0)

    lam = -lam_ref[...]
    softplus = jnp.maximum(lam, 0.0) + jnp.log1p(jnp.exp(-jnp.abs(lam)))
    c_softplus = LRU_C * softplus
    neg_c_softplus_log2e = -LOG2_E * c_softplus
    a_parts, b_parts = [], []
    for j in range(LRU_WIDTH // MXU_DIM):
        cols = slice(j * MXU_DIM, (j + 1) * MXU_DIM)
        pre = _dot(xc[:, cols].astype(_BF16), w_gate_ref[j])
        rg = _sigmoid_of_half(pre[:, :MXU_DIM] + 0.5 * b_rg_ref[:, cols])
        ig = _sigmoid_of_half(pre[:, MXU_DIM:] + 0.5 * b_ig_ref[:, cols])
        a = jnp.exp2(rg * neg_c_softplus_log2e[:, cols])
        a_parts.append(a)
        b_parts.append(_sqrt_nonneg(jnp.tanh(rg * c_softplus[:, cols]) * (1.0 + a * a)) * (ig * xc[:, cols]))
    a_all = jnp.concatenate(a_parts, axis=-1)
    b_all = jnp.concatenate(b_parts, axis=-1)

    local, decay = [], []
    for g in range(SEG_LEN):
        a_g = a_all[g * SUBLANES:(g + 1) * SUBLANES, :]
        b_g = b_all[g * SUBLANES:(g + 1) * SUBLANES, :]
        local.append(b_g if g == 0 else a_g * local[-1] + b_g)
        decay.append(a_g if g == 0 else a_g * decay[-1])
    seg_a, seg_b = decay[-1], local[-1]
    d = 1
    while d < SUBLANES:
        keep = seg_id >= d
        a_prev = jnp.where(keep, pltpu.roll(seg_a, d, axis=0), 1.0)
        b_prev = jnp.where(keep, pltpu.roll(seg_b, d, axis=0), 0.0)
        seg_b = seg_b + seg_a * b_prev
        seg_a = seg_a * a_prev
        d *= 2
    carry = carry_scr[...]
    seg_end = seg_a * carry + seg_b
    seg_in = jnp.where(seg_id == 0, carry, pltpu.roll(seg_end, 1, axis=0))
    carry_scr[...] = seg_end[SUBLANES - 1:SUBLANES, :]
    for g in range(SEG_LEN):
        h_g = local[g] + decay[g] * seg_in
        for slab in range(LRU_SLABS):
            h_scr[slab, pl.ds(g, SUBLANES, stride=SEG_PITCH), :] = h_g[:, slab * LANES:(slab + 1) * LANES]

    h = jnp.concatenate([
        jnp.concatenate([h_scr[slab, seg * SEG_PITCH:seg * SEG_PITCH + SEG_LEN, :] for slab in range(LRU_SLABS)],
                        axis=-1)
        for seg in range(SUBLANES)], axis=0)
    gate_a = proj_scr[:, _P_GATE_A:_P_GATE_A + LRU_WIDTH]
    y_scr[:, 0:LRU_WIDTH] = _gated(h * _row_rms_scale(h), gate_a)

    pos = pl.ds(pl.multiple_of(s * tile, tile), tile)
    cos = cos_ref[pos, :]
    sin = sin_ref[pos, :]
    q_groups = []
    for g in range(SWA_WIDTH // LANES):
        cols = slice(_P_Q + g * LANES, _P_Q + (g + 1) * LANES)
        qraw = proj_scr[:, cols]
        qn = qraw * _head_rms_scale(qraw, head_mean_ref) * q_gain
        q_groups.append(_rope(qn, cos, sin))
    kraw = proj_scr[:, _P_K:_P_K + SWA_KV_WIDTH]
    kn = kraw * _head_rms_scale(kraw, head_mean_ref) * k_gain
    lane_t = lax.broadcasted_iota(jnp.int32, (tile, LANES), 1)
    for name_scr, kv in ((k_scr, _rope(kn, cos, sin)), (v_scr, proj_scr[:, _P_V:_P_V + SWA_KV_WIDTH])):
        swapped = pltpu.roll(kv, HEAD_DIM, axis=1)
        name_scr[0, BLOCK:BLOCK + tile, :] = jnp.where(lane_t < HEAD_DIM, kv, swapped).astype(_BF16)
        name_scr[1, BLOCK:BLOCK + tile, :] = jnp.where(lane_t < HEAD_DIM, swapped, kv).astype(_BF16)

    xq_groups = []
    for g in range(XATTN_WIDTH // LANES):
        cols = slice(_P_XQ + g * LANES, _P_XQ + (g + 1) * LANES)
        xraw = proj_scr[:, cols]
        xq_groups.append(xraw * _head_rms_scale(xraw, head_mean_ref) * xq_gain)

    lane = lax.broadcasted_iota(jnp.int32, (BLOCK, LANES), 1)
    low_half = lane < HEAD_DIM
    half_masks = (low_half, lane >= HEAD_DIM)
    qi = lax.broadcasted_iota(jnp.int32, (BLOCK, BLOCK), 0)
    kj = lax.broadcasted_iota(jnp.int32, (BLOCK, BLOCK), 1)
    own_block = kj <= qi

    for blk in range(n_blocks):
        rows = slice(blk * BLOCK, (blk + 1) * BLOCK)
        o_swa = [None, None]
        for head, (g, hi) in enumerate(_HEAD_SLOTS):
            kv_head = head // (SWA_Q_HEADS // SWA_KV_HEADS)
            kk = k_scr[kv_head, blk * BLOCK:(blk + 2) * BLOCK, :]
            vv = v_scr[kv_head, blk * BLOCK:(blk + 2) * BLOCK, :]
            qm = jnp.where(half_masks[hi], q_groups[g][rows, :], 0.0).astype(_BF16)
            sc_both = _dot_nt(qm, kk)
            sc_prev = sc_both[:, :BLOCK]
            if blk == 0:
                sc_prev = jnp.where(s > 0, sc_prev, NEG_INF)
            sc = jnp.where(own_block, sc_both[:, BLOCK:], sc_prev)
            sink = sinks_ref[layer, head] * LOG2_E
            m = jnp.maximum(jnp.max(sc, axis=-1, keepdims=True), sink)
            p = jnp.exp2(sc - m)
            den = jnp.sum(p, axis=-1, keepdims=True) + jnp.exp2(sink - m)
            p_both = jnp.concatenate([jnp.where(own_block, 0.0, p), jnp.where(own_block, p, 0.0)], axis=-1)
            o = _dot(p_both.astype(_BF16), vv) * (1.0 / den)
            o_swa[g] = o if o_swa[g] is None else jnp.where(low_half, o_swa[g], o)
        y_b = jnp.concatenate(o_swa, axis=-1)
        gate_b = proj_scr[rows, _P_GATE_B:_P_GATE_B + SWA_WIDTH]
        y_scr[rows, LRU_WIDTH:LRU_WIDTH + SWA_WIDTH] = _gated(y_b * _row_rms_scale(y_b), gate_b)

        o_mem = [None, None]
        for head, (g, hi) in enumerate(_HEAD_SLOTS):
            qm = jnp.where(half_masks[hi], xq_groups[g][rows, :], 0.0).astype(_BF16)
            sc = _dot_nt(qm, km_scr[:, g * LANES:(g + 1) * LANES])
            m = jnp.max(sc, axis=-1, keepdims=True)
            p = jnp.exp2(sc - m)
            den = jnp.sum(p, axis=-1, keepdims=True)
            o = _dot(p.astype(_BF16), vm_scr[:, g * LANES:(g + 1) * LANES]) * (1.0 / den)
            o_mem[g] = o if o_mem[g] is None else jnp.where(low_half, o_mem[g], o)
        y_c = jnp.concatenate(o_mem, axis=-1)
        gate_c = proj_scr[rows, _P_GATE_C:_P_GATE_C + XATTN_WIDTH]
        y_scr[rows, LRU_WIDTH + SWA_WIDTH:] = _gated(y_c * _row_rms_scale(y_c), gate_c)

    o_ref[0] = x + _dot(y_scr[...], w_out_ref[...])


def _rope_tables(seq):
    pos = np.arange(seq, dtype=np.float32)
    inv_freq = np.float32(ROPE_THETA) ** (-(np.arange(0, ROPE_DIM, 2, dtype=np.float32) / np.float32(ROPE_DIM)))
    ang = (pos[:, None] * inv_freq[None, :].astype(np.float32)).astype(np.float32)
    cos, sin = np.cos(ang).astype(np.float32), np.sin(ang).astype(np.float32)
    ones = np.ones((seq, HEAD_DIM - ROPE_DIM), np.float32)
    cos_head = np.concatenate([cos, cos, ones], axis=-1)
    sin_head = np.concatenate([-sin, sin, 0.0 * ones], axis=-1)
    reps = LANES // HEAD_DIM
    return jnp.asarray(np.tile(cos_head, (1, reps))), jnp.asarray(np.tile(sin_head, (1, reps)))


def _head_mean_matrix(width):
    block = np.full((HEAD_DIM, HEAD_DIM), 1.0 / HEAD_DIM, np.float32)
    return jnp.asarray(np.kron(np.eye(width // HEAD_DIM, dtype=np.float32), block), dtype=_BF16)


def _const_spec(shape):
    return pl.BlockSpec(shape, lambda *_: (0,) * len(shape))


def _gate_half_columns():
    scale = np.ones((LRU_WIDTH + _P_END,), np.float32)
    for start, width in ((_P_GATE_A, LRU_WIDTH), (_P_GATE_B, SWA_WIDTH), (_P_GATE_C, XATTN_WIDTH)):
        scale[LRU_WIDTH + start:LRU_WIDTH + start + width] = 0.5
    return scale


def _layer(x, mem, p, layer, *, tile):
    B, S, _ = x.shape
    n_tiles = S // tile
    w_in = (p["w_in"][layer] * _gate_half_columns()).astype(_BF16)
    w_out = (p["out_norm_g"][layer][:, None] * p["w_out"][layer]).astype(_BF16)

    def block_diag(w):
        eye = jnp.eye(LRU_BLOCKS, dtype=w.dtype)
        return jnp.einsum("nde,nm->ndme", w, eye).reshape(LRU_WIDTH, LRU_WIDTH)

    w_r, w_i = block_diag(p["w_rg"][layer]), block_diag(p["w_ig"][layer])
    w_gate = (0.5 * jnp.stack([
        jnp.concatenate([w_r[c:c + MXU_DIM, c:c + MXU_DIM], w_i[c:c + MXU_DIM, c:c + MXU_DIM]], axis=-1)
        for c in range(0, LRU_WIDTH, MXU_DIM)])).astype(_BF16)

    cos, sin = _rope_tables(S)
    vector_names = ("mem_norm_g", "xk_norm_g", "norm_g", "conv_w", "conv_b", "b_rg", "b_ig",
                    "lru_lambda", "q_norm_g", "k_norm_g", "xq_norm_g")
    vectors = [p[name].astype(_F32) for name in vector_names]
    w_mem = p["w_mem_kv"][layer].astype(_BF16)
    operands = [p["sinks"].astype(_F32), x, mem, cos, sin] + vectors + [
        w_mem, w_in, w_gate, _head_mean_matrix(LANES), w_out]
    layer_block = lambda a: pl.BlockSpec((1,) + a.shape[1:], lambda b, s: (layer,) + (0,) * (a.ndim - 1))
    in_specs = [
        pl.BlockSpec(memory_space=pltpu.SMEM),
        pl.BlockSpec((1, tile, D_MODEL), lambda b, s: (b, s, 0)),
        pl.BlockSpec((1, MEM_LEN, D_MODEL), lambda b, s: (b, 0, 0)),
        _const_spec(cos.shape), _const_spec(sin.shape),
    ] + [layer_block(v) for v in vectors] + [_const_spec(a.shape) for a in operands[-5:]]

    return pl.pallas_call(
        functools.partial(_layer_kernel, tile=tile, layer=layer),
        out_shape=jax.ShapeDtypeStruct(x.shape, x.dtype),
        grid=(B, n_tiles),
        in_specs=in_specs,
        out_specs=pl.BlockSpec((1, tile, D_MODEL), lambda b, s: (b, s, 0)),
        scratch_shapes=[
            pltpu.VMEM((tile, _P_END), _F32),
            pltpu.VMEM((LRU_SLABS, SUBLANES * SEG_PITCH, LANES), _F32),
            pltpu.VMEM((LRU_SLABS, SUBLANES * SEG_PITCH, LANES), _F32),
            pltpu.VMEM((1, LRU_WIDTH), _F32),
            pltpu.VMEM(((CONV_WIDTH - 1) * SUBLANES, LRU_WIDTH), _F32),
            pltpu.VMEM((SWA_KV_HEADS, tile + BLOCK, LANES), _BF16),
            pltpu.VMEM((SWA_KV_HEADS, tile + BLOCK, LANES), _BF16),
            pltpu.VMEM((tile, D_MIX), _BF16),
            pltpu.VMEM((MEM_LEN, XATTN_WIDTH), _BF16),
            pltpu.VMEM((MEM_LEN, XATTN_WIDTH), _BF16),
        ],
        compiler_params=pltpu.CompilerParams(
            dimension_semantics=("arbitrary", "arbitrary"), vmem_limit_bytes=VMEM_LIMIT_BYTES),
        name="hymba_layer",
    )(*operands)


def kernel(x, mem, norm_g, mem_norm_g, w_in, conv_w, conv_b, w_rg, b_rg, w_ig, b_ig, lru_lambda, q_norm_g,
           k_norm_g, sinks, w_mem_kv, xq_norm_g, xk_norm_g, out_norm_g, w_out):
    depth = w_in.shape[0]
    tile = TILE
    assert x.shape[1] % tile == 0 and tile % BLOCK == 0 and WINDOW == BLOCK
    params = dict(
        norm_g=norm_g, w_in=w_in, conv_w=conv_w, conv_b=conv_b, w_rg=w_rg, b_rg=b_rg, w_ig=w_ig, b_ig=b_ig,
        lru_lambda=lru_lambda, q_norm_g=q_norm_g, k_norm_g=k_norm_g, sinks=sinks, xq_norm_g=xq_norm_g,
        out_norm_g=out_norm_g, w_out=w_out, mem_norm_g=mem_norm_g, w_mem_kv=w_mem_kv, xk_norm_g=xk_norm_g)
    h = x
    for l in range(depth):
        h = _layer(h, mem, params, l, tile=tile)
    return h
```

```python
import functools
import math

import jax
import jax.numpy as jnp
import numpy as np
from jax import lax
from jax.experimental import pallas as pl
from jax.experimental.pallas import tpu as pltpu

D_MODEL = 1024
MEM_LEN = 256
HEAD_DIM = 64
LRU_WIDTH = 512
LRU_BLOCKS = 8
LRU_BLOCK = LRU_WIDTH // LRU_BLOCKS
CONV_WIDTH = 4
LRU_C = 8.0
SWA_Q_HEADS = 4
SWA_KV_HEADS = 2
SWA_WIDTH = SWA_Q_HEADS * HEAD_DIM
SWA_KV_WIDTH = SWA_KV_HEADS * HEAD_DIM
WINDOW = 128
BLOCK = 128
XATTN_HEADS = 4
XATTN_WIDTH = XATTN_HEADS * HEAD_DIM
D_MIX = LRU_WIDTH + SWA_WIDTH + XATTN_WIDTH
ROPE_THETA = 500000.0
ROPE_DIM = HEAD_DIM // 4
EPS = 1e-6
NEG_INF = -1e30
LOG2_E = math.log2(math.e)

LANES = 128
SUBLANES = 8
MXU_DIM = 256
VMEM_LIMIT_BYTES = 56 * 1024 * 1024

TILE = 1024
SEG_LEN = TILE // SUBLANES
SEG_PITCH = SEG_LEN + SUBLANES
LRU_SLABS = LRU_WIDTH // LANES
SQRT_CLAMP = float(np.finfo(np.float32).tiny)

_P_GATE_A = 0
_P_Q = _P_GATE_A + LRU_WIDTH
_P_K = _P_Q + SWA_WIDTH
_P_V = _P_K + SWA_KV_WIDTH
_P_GATE_B = _P_V + SWA_KV_WIDTH
_P_XQ = _P_GATE_B + SWA_WIDTH
_P_GATE_C = _P_XQ + XATTN_WIDTH
_P_END = _P_GATE_C + XATTN_WIDTH

_HEAD_SLOTS = ((0, 0), (0, 1), (1, 0), (1, 1))

_BF16 = jnp.bfloat16
_F32 = jnp.float32


def _dot(a, b):
    return jnp.dot(a, b, preferred_element_type=_F32)


def _dot_nt(a, b):
    return lax.dot_general(a, b, (((1,), (1,)), ((), ())), preferred_element_type=_F32)


def _sigmoid_of_half(hz):
    return 0.5 * jnp.tanh(hz) + 0.5


def _gated(v, hz):
    hz = hz.astype(_BF16)
    return v.astype(_BF16) * (hz + hz * jnp.tanh(hz))


def _sqrt_nonneg(v):
    return v * lax.rsqrt(jnp.maximum(v, SQRT_CLAMP))


def _row_rms_scale(v):
    return lax.rsqrt(jnp.mean(v * v, axis=-1, keepdims=True) + EPS)


def _head_rms_scale(v, head_mean_ref):
    sq = (v * v).astype(_BF16)
    return lax.rsqrt(_dot(sq, head_mean_ref[...]) + EPS)


def _rope(v, cos, sin):
    half = ROPE_DIM // 2
    lane = lax.broadcasted_iota(jnp.int32, v.shape, 1) % HEAD_DIM
    partner = jnp.where(lane < half, pltpu.roll(v, LANES - half, axis=1), pltpu.roll(v, half, axis=1))
    return v * cos + partner * sin


def _layer_kernel(
    sinks_ref, x_ref, mem_ref, cos_ref, sin_ref, mem_g_ref, xkg_ref, norm_g_ref, conv_w3_ref, conv_b_ref,
    b_rg_ref, b_ig_ref, lam_ref, qg_ref, kg_ref, xqg_ref, w_mem_ref, w_in_ref, w_gate_ref, head_mean_ref, w_out_ref,
    o_ref,
    proj_scr, conv_scr, h_scr, carry_scr, tail_scr, k_scr, v_scr, y_scr, km_scr, vm_scr,
    *, tile, layer,
):
    s = pl.program_id(1)
    n_blocks = tile // BLOCK
    conv_w_ref = conv_w3_ref.at[0]
    score_scale = LOG2_E / math.sqrt(HEAD_DIM)
    two_heads = lambda g_ref: jnp.concatenate([g_ref[...]] * (LANES // HEAD_DIM), axis=1)
    q_gain = two_heads(qg_ref) * score_scale
    k_gain = two_heads(kg_ref)
    xq_gain = two_heads(xqg_ref) * score_scale

    @pl.when(s == 0)
    def _reset_sequence_state():
        tail_scr[...] = jnp.zeros_like(tail_scr)
        carry_scr[...] = jnp.zeros_like(carry_scr)
        k_scr[:, tile:tile + BLOCK, :] = jnp.zeros((SWA_KV_HEADS, BLOCK, LANES), _BF16)
        v_scr[:, tile:tile + BLOCK, :] = jnp.zeros((SWA_KV_HEADS, BLOCK, LANES), _BF16)

    @pl.when(s == 0)
    def _project_memory():
        m = mem_ref[0]
        mn = (m * _row_rms_scale(m) * mem_g_ref[...]).astype(_BF16)
        kv = _dot(mn, w_mem_ref[...])
        xk_gain = two_heads(xkg_ref)
        for g in range(XATTN_WIDTH // LANES):
            k = kv[:, g * LANES:(g + 1) * LANES]
            km_scr[:, g * LANES:(g + 1) * LANES] = (k * _head_rms_scale(k, head_mean_ref) * xk_gain).astype(_BF16)
        vm_scr[...] = kv[:, XATTN_WIDTH:].astype(_BF16)

    k_scr[:, 0:BLOCK, :] = k_scr[:, tile:tile + BLOCK, :]
    v_scr[:, 0:BLOCK, :] = v_scr[:, tile:tile + BLOCK, :]

    x = x_ref[0]
    xn = (x * _row_rms_scale(x) * norm_g_ref[...]).astype(_BF16)
    lru_x = _dot(xn, w_in_ref[:, 0:LRU_WIDTH])
    for seg in range(SUBLANES):
        for slab in range(LRU_SLABS):
            conv_scr[slab, seg * SEG_PITCH:seg * SEG_PITCH + SEG_LEN, :] = (
                lru_x[seg * SEG_LEN:(seg + 1) * SEG_LEN, slab * LANES:(slab + 1) * LANES])
    proj_scr[...] = _dot(xn, w_in_ref[:, LRU_WIDTH:])

    seg_id = lax.broadcasted_iota(jnp.int32, (SUBLANES, LRU_WIDTH), 0)

    def strided(ref, g):
        return jnp.concatenate(
            [ref[slab, pl.ds(g, SUBLANES, stride=SEG_PITCH), :] for slab in range(LRU_SLABS)], axis=-1)

    def shift_segments(cur, prev_tile):
        return jnp.where(seg_id == 0, pltpu.roll(prev_tile, 1, axis=0), pltpu.roll(cur, 1, axis=0))

    u = [strided(conv_scr, g) for g in range(SEG_LEN)]
    wrapped = {}
    for k in range(1, CONV_WIDTH):
        wrapped[SEG_LEN - k] = shift_segments(u[SEG_LEN - k], tail_scr[(k - 1) * SUBLANES:k * SUBLANES, :])
    for k in range(1, CONV_WIDTH):
        tail_scr[(k - 1) * SUBLANES:k * SUBLANES, :] = u[SEG_LEN - k]
    xc_parts = []
    for g in range(SEG_LEN):
        xc = conv_b_ref[...] + conv_w_ref[CONV_WIDTH - 1:CONV_WIDTH, :] * u[g]
        for k in range(1, CONV_WIDTH):
            past = u[g - k] if g >= k else wrapped[SEG_LEN + g - k]
            xc = xc + conv_w_ref[CONV_WIDTH - 1 - k:CONV_WIDTH - k, :] * past
        xc_parts.append(xc)
    xc = jnp.concatenate(xc_parts, axis=0)

    lam = -lam_ref[...]
    softplus = jnp.maximum(lam, 0.0) + jnp.log1p(jnp.exp(-jnp.abs(lam)))
    c_softplus = LRU_C * softplus
    neg_c_softplus_log2e = -LOG2_E * c_softplus
    a_parts, b_parts = [], []
    for j in range(LRU_WIDTH // MXU_DIM):
        cols = slice(j * MXU_DIM, (j + 1) * MXU_DIM)
        pre = _dot(xc[:, cols].astype(_BF16), w_gate_ref[j])
        rg = _sigmoid_of_half(pre[:, :MXU_DIM] + 0.5 * b_rg_ref[:, cols])
        ig = _sigmoid_of_half(pre[:, MXU_DIM:] + 0.5 * b_ig_ref[:, cols])
        a = jnp.exp2(rg * neg_c_softplus_log2e[:, cols])
        a_parts.append(a)
        b_parts.append(_sqrt_nonneg(jnp.tanh(rg * c_softplus[:, cols]) * (1.0 + a * a)) * (ig * xc[:, cols]))
    a_all = jnp.concatenate(a_parts, axis=-1)
    b_all = jnp.concatenate(b_parts, axis=-1)

    local, decay = [], []
    for g in range(SEG_LEN):
        a_g = a_all[g * SUBLANES:(g + 1) * SUBLANES, :]
        b_g = b_all[g * SUBLANES:(g + 1) * SUBLANES, :]
        local.append(b_g if g == 0 else a_g * local[-1] + b_g)
        decay.append(a_g if g == 0 else a_g * decay[-1])
    seg_a, seg_b = decay[-1], local[-1]
    d = 1
    while d < SUBLANES:
        keep = seg_id >= d
        a_prev = jnp.where(keep, pltpu.roll(seg_a, d, axis=0), 1.0)
        b_prev = jnp.where(keep, pltpu.roll(seg_b, d, axis=0), 0.0)
        seg_b = seg_b + seg_a * b_prev
        seg_a = seg_a * a_prev
        d *= 2
    carry = carry_scr[...]
    seg_end = seg_a * carry + seg_b
    seg_in = jnp.where(seg_id == 0, carry, pltpu.roll(seg_end, 1, axis=0))
    carry_scr[...] = seg_end[SUBLANES - 1:SUBLANES, :]
    for g in range(SEG_LEN):
        h_g = local[g] + decay[g] * seg_in
        for slab in range(LRU_SLABS):
            h_scr[slab, pl.ds(g, SUBLANES, stride=SEG_PITCH), :] = h_g[:, slab * LANES:(slab + 1) * LANES]

    h = jnp.concatenate([
        jnp.concatenate([h_scr[slab, seg * SEG_PITCH:seg * SEG_PITCH + SEG_LEN, :] for slab in range(LRU_SLABS)],
                        axis=-1)
        for seg in range(SUBLANES)], axis=0)
    gate_a = proj_scr[:, _P_GATE_A:_P_GATE_A + LRU_WIDTH]
    y_scr[:, 0:LRU_WIDTH] = _gated(h * _row_rms_scale(h), gate_a)

    pos = pl.ds(pl.multiple_of(s * tile, tile), tile)
    cos = cos_ref[pos, :]
    sin = sin_ref[pos, :]
    q_groups = []
    for g in range(SWA_WIDTH // LANES):
        cols = slice(_P_Q + g * LANES, _P_Q + (g + 1) * LANES)
        qraw = proj_scr[:, cols]
        qn = qraw * _head_rms_scale(qraw, head_mean_ref) * q_gain
        q_groups.append(_rope(qn, cos, sin))
    kraw = proj_scr[:, _P_K:_P_K + SWA_KV_WIDTH]
    kn = kraw * _head_rms_scale(kraw, head_mean_ref) * k_gain
    for name_scr, kv in ((k_scr, _rope(kn, cos, sin)), (v_scr, proj_scr[:, _P_V:_P_V + SWA_KV_WIDTH])):
        name_scr[0, BLOCK:BLOCK + tile, :] = kv.astype(_BF16)
        name_scr[1, BLOCK:BLOCK + tile, :] = pltpu.roll(kv, HEAD_DIM, axis=1).astype(_BF16)

    xq_groups = []
    for g in range(XATTN_WIDTH // LANES):
        cols = slice(_P_XQ + g * LANES, _P_XQ + (g + 1) * LANES)
        xraw = proj_scr[:, cols]
        xq_groups.append(xraw * _head_rms_scale(xraw, head_mean_ref) * xq_gain)

    lane = lax.broadcasted_iota(jnp.int32, (BLOCK, LANES), 1)
    low_half = lane < HEAD_DIM
    half_masks = (low_half, lane >= HEAD_DIM)
    qi = lax.broadcasted_iota(jnp.int32, (BLOCK, BLOCK), 0)
    kj = lax.broadcasted_iota(jnp.int32, (BLOCK, BLOCK), 1)
    own_block = kj <= qi

    for blk in range(n_blocks):
        rows = slice(blk * BLOCK, (blk + 1) * BLOCK)
        o_swa = [None, None]
        for head, (g, hi) in enumerate(_HEAD_SLOTS):
            kv_copy = int(hi != head // (SWA_Q_HEADS // SWA_KV_HEADS))
            kk = k_scr[kv_copy, blk * BLOCK:(blk + 2) * BLOCK, :]
            vv = v_scr[kv_copy, blk * BLOCK:(blk + 2) * BLOCK, :]
            qm = jnp.where(half_masks[hi], q_groups[g][rows, :], 0.0).astype(_BF16)
            sc_both = _dot_nt(qm, kk)
            sc_prev = sc_both[:, :BLOCK]
            if blk == 0:
                sc_prev = jnp.where(s > 0, sc_prev, NEG_INF)
            sc = jnp.where(own_block, sc_both[:, BLOCK:], sc_prev)
            sink = sinks_ref[layer, head] * LOG2_E
            m = jnp.maximum(jnp.max(sc, axis=-1, keepdims=True), sink)
            p = jnp.exp2(sc - m)
            den = jnp.sum(p, axis=-1, keepdims=True) + jnp.exp2(sink - m)
            p_both = jnp.concatenate([jnp.where(own_block, 0.0, p), jnp.where(own_block, p, 0.0)], axis=-1)
            o = _dot(p_both.astype(_BF16), vv) * (1.0 / den)
            o_swa[g] = o if o_swa[g] is None else jnp.where(low_half, o_swa[g], o)
        y_b = jnp.concatenate(o_swa, axis=-1)
        gate_b = proj_scr[rows, _P_GATE_B:_P_GATE_B + SWA_WIDTH]
        y_scr[rows, LRU_WIDTH:LRU_WIDTH + SWA_WIDTH] = _gated(y_b * _row_rms_scale(y_b), gate_b)

        o_mem = [None, None]
        for head, (g, hi) in enumerate(_HEAD_SLOTS):
            qm = jnp.where(half_masks[hi], xq_groups[g][rows, :], 0.0).astype(_BF16)
            sc = _dot_nt(qm, km_scr[:, g * LANES:(g + 1) * LANES])
            m = jnp.max(sc, axis=-1, keepdims=True)
            p = jnp.exp2(sc - m)
            den = jnp.sum(p, axis=-1, keepdims=True)
            o = _dot(p.astype(_BF16), vm_scr[:, g * LANES:(g + 1) * LANES]) * (1.0 / den)
            o_mem[g] = o if o_mem[g] is None else jnp.where(low_half, o_mem[g], o)
        y_c = jnp.concatenate(o_mem, axis=-1)
        gate_c = proj_scr[rows, _P_GATE_C:_P_GATE_C + XATTN_WIDTH]
        y_scr[rows, LRU_WIDTH + SWA_WIDTH:] = _gated(y_c * _row_rms_scale(y_c), gate_c)

    o_ref[0] = x + _dot(y_scr[...], w_out_ref[...])


def _rope_tables(seq):
    pos = np.arange(seq, dtype=np.float32)
    inv_freq = np.float32(ROPE_THETA) ** (-(np.arange(0, ROPE_DIM, 2, dtype=np.float32) / np.float32(ROPE_DIM)))
    ang = (pos[:, None] * inv_freq[None, :].astype(np.float32)).astype(np.float32)
    cos, sin = np.cos(ang).astype(np.float32), np.sin(ang).astype(np.float32)
    ones = np.ones((seq, HEAD_DIM - ROPE_DIM), np.float32)
    cos_head = np.concatenate([cos, cos, ones], axis=-1)
    sin_head = np.concatenate([-sin, sin, 0.0 * ones], axis=-1)
    reps = LANES // HEAD_DIM
    return jnp.asarray(np.tile(cos_head, (1, reps))), jnp.asarray(np.tile(sin_head, (1, reps)))


def _head_mean_matrix(width):
    block = np.full((HEAD_DIM, HEAD_DIM), 1.0 / HEAD_DIM, np.float32)
    return jnp.asarray(np.kron(np.eye(width // HEAD_DIM, dtype=np.float32), block), dtype=_BF16)


def _const_spec(shape):
    return pl.BlockSpec(shape, lambda *_: (0,) * len(shape))


def _gate_half_columns():
    scale = np.ones((LRU_WIDTH + _P_END,), np.float32)
    for start, width in ((_P_GATE_A, LRU_WIDTH), (_P_GATE_B, SWA_WIDTH), (_P_GATE_C, XATTN_WIDTH)):
        scale[LRU_WIDTH + start:LRU_WIDTH + start + width] = 0.5
    return scale


def _layer(x, mem, p, layer, *, tile):
    B, S, _ = x.shape
    n_tiles = S // tile
    w_in = (p["w_in"][layer] * _gate_half_columns()).astype(_BF16)
    w_out = (p["out_norm_g"][layer][:, None] * p["w_out"][layer]).astype(_BF16)

    def block_diag(w):
        eye = jnp.eye(LRU_BLOCKS, dtype=w.dtype)
        return jnp.einsum("nde,nm->ndme", w, eye).reshape(LRU_WIDTH, LRU_WIDTH)

    w_r, w_i = block_diag(p["w_rg"][layer]), block_diag(p["w_ig"][layer])
    w_gate = (0.5 * jnp.stack([
        jnp.concatenate([w_r[c:c + MXU_DIM, c:c + MXU_DIM], w_i[c:c + MXU_DIM, c:c + MXU_DIM]], axis=-1)
        for c in range(0, LRU_WIDTH, MXU_DIM)])).astype(_BF16)

    cos, sin = _rope_tables(S)
    vector_names = ("mem_norm_g", "xk_norm_g", "norm_g", "conv_w", "conv_b", "b_rg", "b_ig",
                    "lru_lambda", "q_norm_g", "k_norm_g", "xq_norm_g")
    vectors = [p[name].astype(_F32) for name in vector_names]
    w_mem = p["w_mem_kv"][layer].astype(_BF16)
    operands = [p["sinks"].astype(_F32), x, mem, cos, sin] + vectors + [
        w_mem, w_in, w_gate, _head_mean_matrix(LANES), w_out]
    layer_block = lambda a: pl.BlockSpec((1,) + a.shape[1:], lambda b, s: (layer,) + (0,) * (a.ndim - 1))
    in_specs = [
        pl.BlockSpec(memory_space=pltpu.SMEM),
        pl.BlockSpec((1, tile, D_MODEL), lambda b, s: (b, s, 0)),
        pl.BlockSpec((1, MEM_LEN, D_MODEL), lambda b, s: (b, 0, 0)),
        _const_spec(cos.shape), _const_spec(sin.shape),
    ] + [layer_block(v) for v in vectors] + [_const_spec(a.shape) for a in operands[-5:]]

    return pl.pallas_call(
        functools.partial(_layer_kernel, tile=tile, layer=layer),
        out_shape=jax.ShapeDtypeStruct(x.shape, x.dtype),
        grid=(B, n_tiles),
        in_specs=in_specs,
        out_specs=pl.BlockSpec((1, tile, D_MODEL), lambda b, s: (b, s, 0)),
        scratch_shapes=[
            pltpu.VMEM((tile, _P_END), _F32),
            pltpu.VMEM((LRU_SLABS, SUBLANES * SEG_PITCH, LANES), _F32),
            pltpu.VMEM((LRU_SLABS, SUBLANES * SEG_PITCH, LANES), _F32),
            pltpu.VMEM((1, LRU_WIDTH), _F32),
            pltpu.VMEM(((CONV_WIDTH - 1) * SUBLANES, LRU_WIDTH), _F32),
            pltpu.VMEM((SWA_KV_HEADS, tile + BLOCK, LANES), _BF16),
            pltpu.VMEM((SWA_KV_HEADS, tile + BLOCK, LANES), _BF16),
            pltpu.VMEM((tile, D_MIX), _BF16),
            pltpu.VMEM((MEM_LEN, XATTN_WIDTH), _BF16),
            pltpu.VMEM((MEM_LEN, XATTN_WIDTH), _BF16),
        ],
        compiler_params=pltpu.CompilerParams(
            dimension_semantics=("arbitrary", "arbitrary"), vmem_limit_bytes=VMEM_LIMIT_BYTES),
        name="hymba_layer",
    )(*operands)


def kernel(x, mem, norm_g, mem_norm_g, w_in, conv_w, conv_b, w_rg, b_rg, w_ig, b_ig, lru_lambda, q_norm_g,
           k_norm_g, sinks, w_mem_kv, xq_norm_g, xk_norm_g, out_norm_g, w_out):
    depth = w_in.shape[0]
    tile = TILE
    assert x.shape[1] % tile == 0 and tile % BLOCK == 0 and WINDOW == BLOCK
    assert SWA_KV_HEADS == LANES // HEAD_DIM == 2
    params = dict(
        norm_g=norm_g, w_in=w_in, conv_w=conv_w, conv_b=conv_b, w_rg=w_rg, b_rg=b_rg, w_ig=w_ig, b_ig=b_ig,
        lru_lambda=lru_lambda, q_norm_g=q_norm_g, k_norm_g=k_norm_g, sinks=sinks, xq_norm_g=xq_norm_g,
        out_norm_g=out_norm_g, w_out=w_out, mem_norm_g=mem_norm_g, w_mem_kv=w_mem_kv, xk_norm_g=xk_norm_g)
    h = x
    for l in range(depth):
        h = _layer(h, mem, params, l, tile=tile)
    return h
```

```python
import functools
import math

import jax
import jax.numpy as jnp
import numpy as np
from jax import lax
from jax.experimental import pallas as pl
from jax.experimental.pallas import tpu as pltpu

D_MODEL = 1024
MEM_LEN = 256
HEAD_DIM = 64
LRU_WIDTH = 512
LRU_BLOCKS = 8
LRU_BLOCK = LRU_WIDTH // LRU_BLOCKS
CONV_WIDTH = 4
LRU_C = 8.0
SWA_Q_HEADS = 4
SWA_KV_HEADS = 2
SWA_WIDTH = SWA_Q_HEADS * HEAD_DIM
SWA_KV_WIDTH = SWA_KV_HEADS * HEAD_DIM
WINDOW = 128
BLOCK = 128
XATTN_HEADS = 4
XATTN_WIDTH = XATTN_HEADS * HEAD_DIM
D_MIX = LRU_WIDTH + SWA_WIDTH + XATTN_WIDTH
ROPE_THETA = 500000.0
ROPE_DIM = HEAD_DIM // 4
EPS = 1e-6
NEG_INF = -1e30
LOG2_E = math.log2(math.e)

LANES = 128
SUBLANES = 8
MXU_DIM = 256
VMEM_LIMIT_BYTES = 56 * 1024 * 1024

TILE = 1024
SEG_LEN = TILE // SUBLANES
SEG_PITCH = SEG_LEN + SUBLANES
LRU_SLABS = LRU_WIDTH // LANES
SQRT_CLAMP = float(np.finfo(np.float32).tiny)

_P_GATE_A = 0
_P_Q = _P_GATE_A + LRU_WIDTH
_P_K = _P_Q + SWA_WIDTH
_P_V = _P_K + SWA_KV_WIDTH
_P_GATE_B = _P_V + SWA_KV_WIDTH
_P_XQ = _P_GATE_B + SWA_WIDTH
_P_GATE_C = _P_XQ + XATTN_WIDTH
_P_END = _P_GATE_C + XATTN_WIDTH

_HEAD_SLOTS = ((0, 0), (0, 1), (1, 0), (1, 1))

_BF16 = jnp.bfloat16
_F32 = jnp.float32


def _dot(a, b):
    return jnp.dot(a, b, preferred_element_type=_F32)


def _dot_nt(a, b):
    return lax.dot_general(a, b, (((1,), (1,)), ((), ())), preferred_element_type=_F32)


def _sigmoid_of_half(hz):
    return 0.5 * jnp.tanh(hz) + 0.5


def _gated(v, hz):
    hz = hz.astype(_BF16)
    return v.astype(_BF16) * (hz + hz * jnp.tanh(hz))


def _sqrt_nonneg(v):
    return v * lax.rsqrt(jnp.maximum(v, SQRT_CLAMP))


def _row_rms_scale(v):
    return lax.rsqrt(jnp.mean(v * v, axis=-1, keepdims=True) + EPS)


def _head_rms_scale(v, head_mean_ref):
    sq = (v * v).astype(_BF16)
    return lax.rsqrt(_dot(sq, head_mean_ref[...]) + EPS)


def _rope(v, cos, sin):
    half = ROPE_DIM // 2
    lane = lax.broadcasted_iota(jnp.int32, v.shape, 1) % HEAD_DIM
    partner = jnp.where(lane < half, pltpu.roll(v, LANES - half, axis=1), pltpu.roll(v, half, axis=1))
    return v * cos + partner * sin


def _layer_kernel(
    sinks_ref, x_ref, mem_ref, cos_ref, sin_ref, mem_g_ref, xkg_ref, norm_g_ref, conv_w3_ref, conv_b_ref,
    b_rg_ref, b_ig_ref, lam_ref, qg_ref, kg_ref, xqg_ref, out_g_ref, w_rg_ref, w_ig_ref, head_mean_ref,
    w_mem_hbm, w_in_hbm, w_out_hbm,
    o_ref,
    proj_scr, conv_scr, h_scr, carry_scr, tail_scr, k_scr, v_scr, y_scr, km_scr, vm_scr,
    w_mem_ref, w_in_ref, w_gate_ref, w_out_ref, copy_sems,
    *, tile, layer,
):
    s = pl.program_id(1)
    n_blocks = tile // BLOCK

    @pl.when((pl.program_id(0) == 0) & (s == 0))
    def _prepare_weights():
        rows = pl.ds(0, D_MODEL)
        copies = []
        for slab in range(LRU_SLABS):
            copies.append(pltpu.make_async_copy(
                w_in_hbm.at[layer, :, pl.ds(slab * LANES, LANES)], conv_scr.at[slab, rows, :], copy_sems.at[len(copies)]))
        copies.append(pltpu.make_async_copy(
            w_in_hbm.at[layer, :, pl.ds(LRU_WIDTH, _P_END)], proj_scr.at[rows, :], copy_sems.at[len(copies)]))
        copies.append(pltpu.make_async_copy(
            w_out_hbm.at[layer], o_ref.at[0, pl.ds(0, D_MIX), :], copy_sems.at[len(copies)]))
        for slab in range(LRU_SLABS):
            copies.append(pltpu.make_async_copy(
                w_mem_hbm.at[layer, :, pl.ds(slab * LANES, LANES)], h_scr.at[slab, rows, :], copy_sems.at[len(copies)]))
        for copy in copies:
            copy.start()

        spread = (lax.broadcasted_iota(jnp.int32, (LRU_BLOCK, MXU_DIM), 1) % LRU_BLOCK
                  == lax.broadcasted_iota(jnp.int32, (LRU_BLOCK, MXU_DIM), 0)).astype(_BF16)
        on_diagonal = (lax.broadcasted_iota(jnp.int32, (MXU_DIM, MXU_DIM), 0) // LRU_BLOCK
                       == lax.broadcasted_iota(jnp.int32, (MXU_DIM, MXU_DIM), 1) // LRU_BLOCK)
        for part, w_ref in enumerate((w_rg_ref, w_ig_ref)):
            blocks = (0.5 * w_ref[0].reshape(LRU_WIDTH, LRU_BLOCK)).astype(_BF16)
            for j in range(LRU_WIDTH // MXU_DIM):
                wide = _dot(blocks[j * MXU_DIM:(j + 1) * MXU_DIM, :], spread)
                w_gate_ref[j, :, part * MXU_DIM:(part + 1) * MXU_DIM] = jnp.where(on_diagonal, wide, 0.0).astype(_BF16)

        chunk = 2 * LANES
        for slab in range(LRU_SLABS):
            copies[slab].wait()
            cols = slice(slab * LANES, (slab + 1) * LANES)
            w_in_ref[:, cols] = _scale_gate_columns(conv_scr[slab, 0:D_MODEL, :], slab * LANES).astype(_BF16)
        copies[LRU_SLABS].wait()
        for r0 in range(0, D_MODEL, chunk):
            w_in_ref[r0:r0 + chunk, LRU_WIDTH:] = _scale_gate_columns(
                proj_scr[r0:r0 + chunk, :], LRU_WIDTH).astype(_BF16)
        copies[LRU_SLABS + 1].wait()
        for r0 in range(0, D_MIX, LANES):
            gain_col = jnp.broadcast_to(out_g_ref[:, r0:r0 + LANES], (LANES, LANES)).T
            gain = jnp.concatenate([gain_col] * (D_MODEL // LANES), axis=-1)
            w_out_ref[r0:r0 + LANES, :] = (gain * o_ref[0, r0:r0 + LANES, :]).astype(_BF16)
        for slab in range(LRU_SLABS):
            copies[LRU_SLABS + 2 + slab].wait()
            cols = slice(slab * LANES, (slab + 1) * LANES)
            w_mem_ref[:, cols] = h_scr[slab, 0:D_MODEL, :].astype(_BF16)
    conv_w_ref = conv_w3_ref.at[0]
    score_scale = LOG2_E / math.sqrt(HEAD_DIM)
    two_heads = lambda g_ref: jnp.concatenate([g_ref[...]] * (LANES // HEAD_DIM), axis=1)
    q_gain = two_heads(qg_ref) * score_scale
    k_gain = two_heads(kg_ref)
    xq_gain = two_heads(xqg_ref) * score_scale

    @pl.when(s == 0)
    def _reset_sequence_state():
        tail_scr[...] = jnp.zeros_like(tail_scr)
        carry_scr[...] = jnp.zeros_like(carry_scr)
        k_scr[:, tile:tile + BLOCK, :] = jnp.zeros((SWA_KV_HEADS, BLOCK, LANES), _BF16)
        v_scr[:, tile:tile + BLOCK, :] = jnp.zeros((SWA_KV_HEADS, BLOCK, LANES), _BF16)

    @pl.when(s == 0)
    def _project_memory():
        m = mem_ref[0]
        mn = (m * _row_rms_scale(m) * mem_g_ref[...]).astype(_BF16)
        kv = _dot(mn, w_mem_ref[...])
        xk_gain = two_heads(xkg_ref)
        for g in range(XATTN_WIDTH // LANES):
            k = kv[:, g * LANES:(g + 1) * LANES]
            km_scr[:, g * LANES:(g + 1) * LANES] = (k * _head_rms_scale(k, head_mean_ref) * xk_gain).astype(_BF16)
        vm_scr[...] = kv[:, XATTN_WIDTH:].astype(_BF16)

    k_scr[:, 0:BLOCK, :] = k_scr[:, tile:tile + BLOCK, :]
    v_scr[:, 0:BLOCK, :] = v_scr[:, tile:tile + BLOCK, :]

    x = x_ref[0]
    xn = (x * _row_rms_scale(x) * norm_g_ref[...]).astype(_BF16)
    lru_x = _dot(xn, w_in_ref[:, 0:LRU_WIDTH])
    for seg in range(SUBLANES):
        for slab in range(LRU_SLABS):
            conv_scr[slab, seg * SEG_PITCH:seg * SEG_PITCH + SEG_LEN, :] = (
                lru_x[seg * SEG_LEN:(seg + 1) * SEG_LEN, slab * LANES:(slab + 1) * LANES])
    proj_scr[...] = _dot(xn, w_in_ref[:, LRU_WIDTH:])

    seg_id = lax.broadcasted_iota(jnp.int32, (SUBLANES, LRU_WIDTH), 0)

    def strided(ref, g):
        return jnp.concatenate(
            [ref[slab, pl.ds(g, SUBLANES, stride=SEG_PITCH), :] for slab in range(LRU_SLABS)], axis=-1)

    def shift_segments(cur, prev_tile):
        return jnp.where(seg_id == 0, pltpu.roll(prev_tile, 1, axis=0), pltpu.roll(cur, 1, axis=0))

    u = [strided(conv_scr, g) for g in range(SEG_LEN)]
    wrapped = {}
    for k in range(1, CONV_WIDTH):
        wrapped[SEG_LEN - k] = shift_segments(u[SEG_LEN - k], tail_scr[(k - 1) * SUBLANES:k * SUBLANES, :])
    for k in range(1, CONV_WIDTH):
        tail_scr[(k - 1) * SUBLANES:k * SUBLANES, :] = u[SEG_LEN - k]
    xc_parts = []
    for g in range(SEG_LEN):
        xc = conv_b_ref[...] + conv_w_ref[CONV_WIDTH - 1:CONV_WIDTH, :] * u[g]
        for k in range(1, CONV_WIDTH):
            past = u[g - k] if g >= k else wrapped[SEG_LEN + g - k]
            xc = xc + conv_w_ref[CONV_WIDTH - 1 - k:CONV_WIDTH - k, :] * past
        xc_parts.append(xc)
    xc = jnp.concatenate(xc_parts, axis=0)

    lam = -lam_ref[...]
    softplus = jnp.maximum(lam, 0.0) + jnp.log1p(jnp.exp(-jnp.abs(lam)))
    c_softplus = LRU_C * softplus
    neg_c_softplus_log2e = -LOG2_E * c_softplus
    a_parts, b_parts = [], []
    for j in range(LRU_WIDTH // MXU_DIM):
        cols = slice(j * MXU_DIM, (j + 1) * MXU_DIM)
        pre = _dot(xc[:, cols].astype(_BF16), w_gate_ref[j])
        rg = _sigmoid_of_half(pre[:, :MXU_DIM] + 0.5 * b_rg_ref[:, cols])
        ig = _sigmoid_of_half(pre[:, MXU_DIM:] + 0.5 * b_ig_ref[:, cols])
        a = jnp.exp2(rg * neg_c_softplus_log2e[:, cols])
        a_parts.append(a)
        b_parts.append(_sqrt_nonneg(jnp.tanh(rg * c_softplus[:, cols]) * (1.0 + a * a)) * (ig * xc[:, cols]))
    a_all = jnp.concatenate(a_parts, axis=-1)
    b_all = jnp.concatenate(b_parts, axis=-1)

    local, decay = [], []
    for g in range(SEG_LEN):
        a_g = a_all[g * SUBLANES:(g + 1) * SUBLANES, :]
        b_g = b_all[g * SUBLANES:(g + 1) * SUBLANES, :]
        local.append(b_g if g == 0 else a_g * local[-1] + b_g)
        decay.append(a_g if g == 0 else a_g * decay[-1])
    seg_a, seg_b = decay[-1], local[-1]
    d = 1
    while d < SUBLANES:
        keep = seg_id >= d
        a_prev = jnp.where(keep, pltpu.roll(seg_a, d, axis=0), 1.0)
        b_prev = jnp.where(keep, pltpu.roll(seg_b, d, axis=0), 0.0)
        seg_b = seg_b + seg_a * b_prev
        seg_a = seg_a * a_prev
        d *= 2
    carry = carry_scr[...]
    seg_end = seg_a * carry + seg_b
    seg_in = jnp.where(seg_id == 0, carry, pltpu.roll(seg_end, 1, axis=0))
    carry_scr[...] = seg_end[SUBLANES - 1:SUBLANES, :]
    for g in range(SEG_LEN):
        h_g = local[g] + decay[g] * seg_in
        for slab in range(LRU_SLABS):
            h_scr[slab, pl.ds(g, SUBLANES, stride=SEG_PITCH), :] = h_g[:, slab * LANES:(slab + 1) * LANES]

    h = jnp.concatenate([
        jnp.concatenate([h_scr[slab, seg * SEG_PITCH:seg * SEG_PITCH + SEG_LEN, :] for slab in range(LRU_SLABS)],
                        axis=-1)
        for seg in range(SUBLANES)], axis=0)
    gate_a = proj_scr[:, _P_GATE_A:_P_GATE_A + LRU_WIDTH]
    y_scr[:, 0:LRU_WIDTH] = _gated(h * _row_rms_scale(h), gate_a)

    pos = pl.ds(pl.multiple_of(s * tile, tile), tile)
    cos = cos_ref[pos, :]
    sin = sin_ref[pos, :]
    q_groups = []
    for g in range(SWA_WIDTH // LANES):
        cols = slice(_P_Q + g * LANES, _P_Q + (g + 1) * LANES)
        qraw = proj_scr[:, cols]
        qn = qraw * _head_rms_scale(qraw, head_mean_ref) * q_gain
        q_groups.append(_rope(qn, cos, sin))
    kraw = proj_scr[:, _P_K:_P_K + SWA_KV_WIDTH]
    kn = kraw * _head_rms_scale(kraw, head_mean_ref) * k_gain
    for name_scr, kv in ((k_scr, _rope(kn, cos, sin)), (v_scr, proj_scr[:, _P_V:_P_V + SWA_KV_WIDTH])):
        name_scr[0, BLOCK:BLOCK + tile, :] = kv.astype(_BF16)
        name_scr[1, BLOCK:BLOCK + tile, :] = pltpu.roll(kv, HEAD_DIM, axis=1).astype(_BF16)

    xq_groups = []
    for g in range(XATTN_WIDTH // LANES):
        cols = slice(_P_XQ + g * LANES, _P_XQ + (g + 1) * LANES)
        xraw = proj_scr[:, cols]
        xq_groups.append(xraw * _head_rms_scale(xraw, head_mean_ref) * xq_gain)

    lane = lax.broadcasted_iota(jnp.int32, (BLOCK, LANES), 1)
    low_half = lane < HEAD_DIM
    half_masks = (low_half, lane >= HEAD_DIM)
    qi = lax.broadcasted_iota(jnp.int32, (BLOCK, BLOCK), 0)
    kj = lax.broadcasted_iota(jnp.int32, (BLOCK, BLOCK), 1)
    own_block = kj <= qi

    for blk in range(n_blocks):
        rows = slice(blk * BLOCK, (blk + 1) * BLOCK)
        o_swa = [None, None]
        for head, (g, hi) in enumerate(_HEAD_SLOTS):
            kv_copy = int(hi != head // (SWA_Q_HEADS // SWA_KV_HEADS))
            kk = k_scr[kv_copy, blk * BLOCK:(blk + 2) * BLOCK, :]
            vv = v_scr[kv_copy, blk * BLOCK:(blk + 2) * BLOCK, :]
            qm = jnp.where(half_masks[hi], q_groups[g][rows, :], 0.0).astype(_BF16)
            sc_both = _dot_nt(qm, kk)
            sc_prev = sc_both[:, :BLOCK]
            if blk == 0:
                sc_prev = jnp.where(s > 0, sc_prev, NEG_INF)
            sc = jnp.where(own_block, sc_both[:, BLOCK:], sc_prev)
            sink = sinks_ref[layer, head] * LOG2_E
            m = jnp.maximum(jnp.max(sc, axis=-1, keepdims=True), sink)
            p = jnp.exp2(sc - m)
            den = jnp.sum(p, axis=-1, keepdims=True) + jnp.exp2(sink - m)
            p_both = jnp.concatenate([jnp.where(own_block, 0.0, p), jnp.where(own_block, p, 0.0)], axis=-1)
            o = _dot(p_both.astype(_BF16), vv) * (1.0 / den)
            o_swa[g] = o if o_swa[g] is None else jnp.where(low_half, o_swa[g], o)
        y_b = jnp.concatenate(o_swa, axis=-1)
        gate_b = proj_scr[rows, _P_GATE_B:_P_GATE_B + SWA_WIDTH]
        y_scr[rows, LRU_WIDTH:LRU_WIDTH + SWA_WIDTH] = _gated(y_b * _row_rms_scale(y_b), gate_b)

        o_mem = [None, None]
        for head, (g, hi) in enumerate(_HEAD_SLOTS):
            qm = jnp.where(half_masks[hi], xq_groups[g][rows, :], 0.0).astype(_BF16)
            sc = _dot_nt(qm, km_scr[:, g * LANES:(g + 1) * LANES])
            m = jnp.max(sc, axis=-1, keepdims=True)
            p = jnp.exp2(sc - m)
            den = jnp.sum(p, axis=-1, keepdims=True)
            o = _dot(p.astype(_BF16), vm_scr[:, g * LANES:(g + 1) * LANES]) * (1.0 / den)
            o_mem[g] = o if o_mem[g] is None else jnp.where(low_half, o_mem[g], o)
        y_c = jnp.concatenate(o_mem, axis=-1)
        gate_c = proj_scr[rows, _P_GATE_C:_P_GATE_C + XATTN_WIDTH]
        y_scr[rows, LRU_WIDTH + SWA_WIDTH:] = _gated(y_c * _row_rms_scale(y_c), gate_c)

    o_ref[0] = x + _dot(y_scr[...], w_out_ref[...])


def _rope_tables(seq):
    pos = np.arange(seq, dtype=np.float32)
    inv_freq = np.float32(ROPE_THETA) ** (-(np.arange(0, ROPE_DIM, 2, dtype=np.float32) / np.float32(ROPE_DIM)))
    ang = (pos[:, None] * inv_freq[None, :].astype(np.float32)).astype(np.float32)
    cos, sin = np.cos(ang).astype(np.float32), np.sin(ang).astype(np.float32)
    ones = np.ones((seq, HEAD_DIM - ROPE_DIM), np.float32)
    cos_head = np.concatenate([cos, cos, ones], axis=-1)
    sin_head = np.concatenate([-sin, sin, 0.0 * ones], axis=-1)
    reps = LANES // HEAD_DIM
    return jnp.asarray(np.tile(cos_head, (1, reps))), jnp.asarray(np.tile(sin_head, (1, reps)))


def _head_mean_matrix(width):
    block = np.full((HEAD_DIM, HEAD_DIM), 1.0 / HEAD_DIM, np.float32)
    return jnp.asarray(np.kron(np.eye(width // HEAD_DIM, dtype=np.float32), block), dtype=_BF16)


def _const_spec(shape):
    return pl.BlockSpec(shape, lambda *_: (0,) * len(shape))


def _gate_half_columns():
    scale = np.ones((LRU_WIDTH + _P_END,), np.float32)
    for start, width in ((_P_GATE_A, LRU_WIDTH), (_P_GATE_B, SWA_WIDTH), (_P_GATE_C, XATTN_WIDTH)):
        scale[LRU_WIDTH + start:LRU_WIDTH + start + width] = 0.5
    return scale


def _scale_gate_columns(w, first_col):
    half = _gate_half_columns()
    pieces = []
    for c in range(0, w.shape[1], LANES):
        scale = half[first_col + c:first_col + c + LANES]
        assert (scale == scale[0]).all()
        piece = w[:, c:c + LANES]
        pieces.append(piece if scale[0] == 1.0 else piece * float(scale[0]))
    return jnp.concatenate(pieces, axis=-1)


def _layer(x, mem, p, layer, *, tile):
    B, S, _ = x.shape
    n_tiles = S // tile
    assert tile >= D_MODEL and tile >= D_MIX and 2 * XATTN_WIDTH == LRU_WIDTH
    cos, sin = _rope_tables(S)
    vector_names = ("mem_norm_g", "xk_norm_g", "norm_g", "conv_w", "conv_b", "b_rg", "b_ig",
                    "lru_lambda", "q_norm_g", "k_norm_g", "xq_norm_g", "out_norm_g", "w_rg", "w_ig")
    vectors = [p[name].astype(_F32) for name in vector_names]
    big_weights = [p[name].astype(_F32) for name in ("w_mem_kv", "w_in", "w_out")]
    head_mean = _head_mean_matrix(LANES)
    operands = [p["sinks"].astype(_F32), x, mem, cos, sin] + vectors + [head_mean] + big_weights
    layer_block = lambda a: pl.BlockSpec((1,) + a.shape[1:], lambda b, s: (layer,) + (0,) * (a.ndim - 1))
    in_specs = [
        pl.BlockSpec(memory_space=pltpu.SMEM),
        pl.BlockSpec((1, tile, D_MODEL), lambda b, s: (b, s, 0)),
        pl.BlockSpec((1, MEM_LEN, D_MODEL), lambda b, s: (b, 0, 0)),
        _const_spec(cos.shape), _const_spec(sin.shape),
    ] + [layer_block(v) for v in vectors] + [_const_spec(head_mean.shape)] + [
        pl.BlockSpec(memory_space=pl.ANY) for _ in big_weights]

    return pl.pallas_call(
        functools.partial(_layer_kernel, tile=tile, layer=layer),
        out_shape=jax.ShapeDtypeStruct(x.shape, x.dtype),
        grid=(B, n_tiles),
        in_specs=in_specs,
        out_specs=pl.BlockSpec((1, tile, D_MODEL), lambda b, s: (b, s, 0)),
        scratch_shapes=[
            pltpu.VMEM((tile, _P_END), _F32),
            pltpu.VMEM((LRU_SLABS, SUBLANES * SEG_PITCH, LANES), _F32),
            pltpu.VMEM((LRU_SLABS, SUBLANES * SEG_PITCH, LANES), _F32),
            pltpu.VMEM((1, LRU_WIDTH), _F32),
            pltpu.VMEM(((CONV_WIDTH - 1) * SUBLANES, LRU_WIDTH), _F32),
            pltpu.VMEM((SWA_KV_HEADS, tile + BLOCK, LANES), _BF16),
            pltpu.VMEM((SWA_KV_HEADS, tile + BLOCK, LANES), _BF16),
            pltpu.VMEM((tile, D_MIX), _BF16),
            pltpu.VMEM((MEM_LEN, XATTN_WIDTH), _BF16),
            pltpu.VMEM((MEM_LEN, XATTN_WIDTH), _BF16),
            pltpu.VMEM((D_MODEL, 2 * XATTN_WIDTH), _BF16),
            pltpu.VMEM((D_MODEL, LRU_WIDTH + _P_END), _BF16),
            pltpu.VMEM((LRU_WIDTH // MXU_DIM, MXU_DIM, 2 * MXU_DIM), _BF16),
            pltpu.VMEM((D_MIX, D_MODEL), _BF16),
            pltpu.SemaphoreType.DMA((2 * LRU_SLABS + 2,)),
        ],
        compiler_params=pltpu.CompilerParams(
            dimension_semantics=("arbitrary", "arbitrary"), vmem_limit_bytes=VMEM_LIMIT_BYTES),
        name="hymba_layer",
    )(*operands)


def kernel(x, mem, norm_g, mem_norm_g, w_in, conv_w, conv_b, w_rg, b_rg, w_ig, b_ig, lru_lambda, q_norm_g,
           k_norm_g, sinks, w_mem_kv, xq_norm_g, xk_norm_g, out_norm_g, w_out):
    depth = w_in.shape[0]
    tile = TILE
    assert x.shape[1] % tile == 0 and tile % BLOCK == 0 and WINDOW == BLOCK
    assert SWA_KV_HEADS == LANES // HEAD_DIM == 2
    params = dict(
        norm_g=norm_g, w_in=w_in, conv_w=conv_w, conv_b=conv_b, w_rg=w_rg, b_rg=b_rg, w_ig=w_ig, b_ig=b_ig,
        lru_lambda=lru_lambda, q_norm_g=q_norm_g, k_norm_g=k_norm_g, sinks=sinks, xq_norm_g=xq_norm_g,
        out_norm_g=out_norm_g, w_out=w_out, mem_norm_g=mem_norm_g, w_mem_kv=w_mem_kv, xk_norm_g=xk_norm_g)
    h = x
    for l in range(depth):
        h = _layer(h, mem, params, l, tile=tile)
    return h
```

```python
import functools
import math

import jax
import jax.numpy as jnp
import numpy as np
from jax import lax
from jax.experimental import pallas as pl
from jax.experimental.pallas import tpu as pltpu

D_MODEL = 1024
MEM_LEN = 256
HEAD_DIM = 64
LRU_WIDTH = 512
LRU_BLOCKS = 8
LRU_BLOCK = LRU_WIDTH // LRU_BLOCKS
CONV_WIDTH = 4
LRU_C = 8.0
SWA_Q_HEADS = 4
SWA_KV_HEADS = 2
SWA_WIDTH = SWA_Q_HEADS * HEAD_DIM
SWA_KV_WIDTH = SWA_KV_HEADS * HEAD_DIM
WINDOW = 128
BLOCK = 128
XATTN_HEADS = 4
XATTN_WIDTH = XATTN_HEADS * HEAD_DIM
D_MIX = LRU_WIDTH + SWA_WIDTH + XATTN_WIDTH
ROPE_THETA = 500000.0
ROPE_DIM = HEAD_DIM // 4
EPS = 1e-6
NEG_INF = -1e30
LOG2_E = math.log2(math.e)

LANES = 128
SUBLANES = 8
MXU_DIM = 256
VMEM_LIMIT_BYTES = 56 * 1024 * 1024

TILE = 1024
SEG_LEN = TILE // SUBLANES
SEG_PITCH = SEG_LEN + SUBLANES
LRU_SLABS = LRU_WIDTH // LANES
SQRT_CLAMP = float(np.finfo(np.float32).tiny)

_P_GATE_A = 0
_P_Q = _P_GATE_A + LRU_WIDTH
_P_K = _P_Q + SWA_WIDTH
_P_V = _P_K + SWA_KV_WIDTH
_P_GATE_B = _P_V + SWA_KV_WIDTH
_P_XQ = _P_GATE_B + SWA_WIDTH
_P_GATE_C = _P_XQ + XATTN_WIDTH
_P_END = _P_GATE_C + XATTN_WIDTH

_HEAD_SLOTS = ((0, 0), (0, 1), (1, 0), (1, 1))

_BF16 = jnp.bfloat16
_F32 = jnp.float32


def _dot(a, b):
    return jnp.dot(a, b, preferred_element_type=_F32)


def _dot_nt(a, b):
    return lax.dot_general(a, b, (((1,), (1,)), ((), ())), preferred_element_type=_F32)


def _sigmoid_of_half(hz):
    return 0.5 * jnp.tanh(hz) + 0.5


def _gated(v, hz):
    hz = hz.astype(_BF16)
    return v.astype(_BF16) * (hz + hz * jnp.tanh(hz))


def _sqrt_nonneg(v):
    return v * lax.rsqrt(jnp.maximum(v, SQRT_CLAMP))


def _row_rms_scale(v):
    return lax.rsqrt(jnp.mean(v * v, axis=-1, keepdims=True) + EPS)


def _head_rms_scale(v, head_mean_ref):
    sq = (v * v).astype(_BF16)
    return lax.rsqrt(_dot(sq, head_mean_ref[...]) + EPS)


def _rope(v, cos, sin):
    half = ROPE_DIM // 2
    lane = lax.broadcasted_iota(jnp.int32, v.shape, 1) % HEAD_DIM
    partner = jnp.where(lane < half, pltpu.roll(v, LANES - half, axis=1), pltpu.roll(v, half, axis=1))
    return v * cos + partner * sin


def _layer_kernel(
    sinks_ref, x_ref, mem_ref, cos_ref, sin_ref, mem_g_ref, xkg_ref, norm_g_ref, conv_w3_ref, conv_b_ref,
    b_rg_ref, b_ig_ref, lam_ref, qg_ref, kg_ref, xqg_ref, out_g_ref, w_rg_ref, w_ig_ref, head_mean_ref,
    w_mem_hbm, w_in_hbm, w_out_hbm,
    o_ref,
    proj_scr, conv_scr, h_scr, carry_scr, tail_scr, k_scr, v_scr, y_scr, km_scr, vm_scr,
    w_mem_ref, w_in_ref, w_gate_ref, w_out_ref, copy_sems,
    *, tile, layer,
):
    s = pl.program_id(1)
    n_blocks = tile // BLOCK

    first_step = (pl.program_id(0) == 0) & (s == 0)

    def weight_copies():
        rows = pl.ds(0, D_MODEL)
        copies = []
        for slab in range(LRU_SLABS):
            copies.append(pltpu.make_async_copy(
                w_mem_hbm.at[layer, :, pl.ds(slab * LANES, LANES)], h_scr.at[slab, rows, :], copy_sems.at[len(copies)]))
        for slab in range(LRU_SLABS):
            copies.append(pltpu.make_async_copy(
                w_in_hbm.at[layer, :, pl.ds(slab * LANES, LANES)], conv_scr.at[slab, rows, :], copy_sems.at[len(copies)]))
        copies.append(pltpu.make_async_copy(
            w_in_hbm.at[layer, :, pl.ds(LRU_WIDTH, _P_END)], proj_scr.at[rows, :], copy_sems.at[len(copies)]))
        copies.append(pltpu.make_async_copy(
            w_out_hbm.at[layer], o_ref.at[0, pl.ds(0, D_MIX), :], copy_sems.at[len(copies)]))
        return copies[:LRU_SLABS], copies[LRU_SLABS:2 * LRU_SLABS + 1], copies[-1]

    def gain_columns(g_ref, r0, width):
        col = jnp.broadcast_to(g_ref[:, r0:r0 + LANES], (LANES, LANES)).T
        return jnp.concatenate([col] * (width // LANES), axis=-1)

    @pl.when(first_step)
    def _prepare_weights():
        mem_copies, in_copies, out_copy = weight_copies()
        for copy in mem_copies + in_copies + [out_copy]:
            copy.start()

        spread = (lax.broadcasted_iota(jnp.int32, (LRU_BLOCK, MXU_DIM), 1) % LRU_BLOCK
                  == lax.broadcasted_iota(jnp.int32, (LRU_BLOCK, MXU_DIM), 0)).astype(_BF16)
        on_diagonal = (lax.broadcasted_iota(jnp.int32, (MXU_DIM, MXU_DIM), 0) // LRU_BLOCK
                       == lax.broadcasted_iota(jnp.int32, (MXU_DIM, MXU_DIM), 1) // LRU_BLOCK)
        for part, w_ref in enumerate((w_rg_ref, w_ig_ref)):
            blocks = (0.5 * w_ref[0].reshape(LRU_WIDTH, LRU_BLOCK)).astype(_BF16)
            for j in range(LRU_WIDTH // MXU_DIM):
                wide = _dot(blocks[j * MXU_DIM:(j + 1) * MXU_DIM, :], spread)
                w_gate_ref[j, :, part * MXU_DIM:(part + 1) * MXU_DIM] = jnp.where(on_diagonal, wide, 0.0).astype(_BF16)

        for slab, copy in enumerate(mem_copies):
            copy.wait()
            cols = slice(slab * LANES, (slab + 1) * LANES)
            w_mem_ref[:, cols] = h_scr[slab, 0:D_MODEL, :].astype(_BF16)
        for copy in in_copies:
            copy.wait()
        for r0 in range(0, D_MODEL, LANES):
            lru_cols = jnp.concatenate([conv_scr[slab, r0:r0 + LANES, :] for slab in range(LRU_SLABS)], axis=-1)
            w_rows = jnp.concatenate([_scale_gate_columns(lru_cols, 0),
                                      _scale_gate_columns(proj_scr[r0:r0 + LANES, :], LRU_WIDTH)], axis=-1)
            w_in_ref[r0:r0 + LANES, :] = (gain_columns(norm_g_ref, r0, LRU_WIDTH + _P_END) * w_rows).astype(_BF16)
        out_copy.wait()
        for r0 in range(0, D_MIX, LANES):
            w_out_ref[r0:r0 + LANES, :] = (gain_columns(out_g_ref, r0, D_MODEL) * o_ref[0, r0:r0 + LANES, :]).astype(_BF16)

    conv_w_ref = conv_w3_ref.at[0]
    score_scale = LOG2_E / math.sqrt(HEAD_DIM)
    two_heads = lambda g_ref: jnp.concatenate([g_ref[...]] * (LANES // HEAD_DIM), axis=1)
    q_gain = two_heads(qg_ref) * score_scale
    k_gain = two_heads(kg_ref)
    xq_gain = two_heads(xqg_ref) * score_scale

    @pl.when(s == 0)
    def _reset_sequence_state():
        tail_scr[...] = jnp.zeros_like(tail_scr)
        carry_scr[...] = jnp.zeros_like(carry_scr)
        k_scr[:, tile:tile + BLOCK, :] = jnp.zeros((SWA_KV_HEADS, BLOCK, LANES), _BF16)
        v_scr[:, tile:tile + BLOCK, :] = jnp.zeros((SWA_KV_HEADS, BLOCK, LANES), _BF16)

    @pl.when(s == 0)
    def _project_memory():
        m = mem_ref[0]
        mn = (m * _row_rms_scale(m) * mem_g_ref[...]).astype(_BF16)
        kv = _dot(mn, w_mem_ref[...])
        xk_gain = two_heads(xkg_ref)
        for g in range(XATTN_WIDTH // LANES):
            k = kv[:, g * LANES:(g + 1) * LANES]
            km_scr[:, g * LANES:(g + 1) * LANES] = (k * _head_rms_scale(k, head_mean_ref) * xk_gain).astype(_BF16)
        vm_scr[...] = kv[:, XATTN_WIDTH:].astype(_BF16)

    k_scr[:, 0:BLOCK, :] = k_scr[:, tile:tile + BLOCK, :]
    v_scr[:, 0:BLOCK, :] = v_scr[:, tile:tile + BLOCK, :]

    x = x_ref[0]
    xn = (x * _row_rms_scale(x)).astype(_BF16)
    lru_x = _dot(xn, w_in_ref[:, 0:LRU_WIDTH])
    for seg in range(SUBLANES):
        for slab in range(LRU_SLABS):
            conv_scr[slab, seg * SEG_PITCH:seg * SEG_PITCH + SEG_LEN, :] = (
                lru_x[seg * SEG_LEN:(seg + 1) * SEG_LEN, slab * LANES:(slab + 1) * LANES])
    proj_scr[...] = _dot(xn, w_in_ref[:, LRU_WIDTH:])

    seg_id = lax.broadcasted_iota(jnp.int32, (SUBLANES, LRU_WIDTH), 0)

    def strided(ref, g):
        return jnp.concatenate(
            [ref[slab, pl.ds(g, SUBLANES, stride=SEG_PITCH), :] for slab in range(LRU_SLABS)], axis=-1)

    def shift_segments(cur, prev_tile):
        return jnp.where(seg_id == 0, pltpu.roll(prev_tile, 1, axis=0), pltpu.roll(cur, 1, axis=0))

    u = [strided(conv_scr, g) for g in range(SEG_LEN)]
    wrapped = {}
    for k in range(1, CONV_WIDTH):
        wrapped[SEG_LEN - k] = shift_segments(u[SEG_LEN - k], tail_scr[(k - 1) * SUBLANES:k * SUBLANES, :])
    for k in range(1, CONV_WIDTH):
        tail_scr[(k - 1) * SUBLANES:k * SUBLANES, :] = u[SEG_LEN - k]
    xc_parts = []
    for g in range(SEG_LEN):
        xc = conv_b_ref[...] + conv_w_ref[CONV_WIDTH - 1:CONV_WIDTH, :] * u[g]
        for k in range(1, CONV_WIDTH):
            past = u[g - k] if g >= k else wrapped[SEG_LEN + g - k]
            xc = xc + conv_w_ref[CONV_WIDTH - 1 - k:CONV_WIDTH - k, :] * past
        xc_parts.append(xc)
    xc = jnp.concatenate(xc_parts, axis=0)

    lam = -lam_ref[...]
    softplus = jnp.maximum(lam, 0.0) + jnp.log1p(jnp.exp(-jnp.abs(lam)))
    half_c_softplus = 0.5 * LRU_C * softplus
    a_parts, b_parts = [], []
    for j in range(LRU_WIDTH // MXU_DIM):
        cols = slice(j * MXU_DIM, (j + 1) * MXU_DIM)
        pre = _dot(xc[:, cols].astype(_BF16), w_gate_ref[j])
        r_tanh = jnp.tanh(pre[:, :MXU_DIM] + 0.5 * b_rg_ref[:, cols])
        ig = _sigmoid_of_half(pre[:, MXU_DIM:] + 0.5 * b_ig_ref[:, cols])
        neg_log_a = r_tanh * half_c_softplus[:, cols] + half_c_softplus[:, cols]
        a = jnp.exp2(neg_log_a * -LOG2_E)
        a_parts.append(a)
        b_parts.append(_sqrt_nonneg(jnp.tanh(neg_log_a) * (1.0 + a * a)) * (ig * xc[:, cols]))
    a_all = jnp.concatenate(a_parts, axis=-1)
    b_all = jnp.concatenate(b_parts, axis=-1)

    local, decay = [], []
    for g in range(SEG_LEN):
        a_g = a_all[g * SUBLANES:(g + 1) * SUBLANES, :]
        b_g = b_all[g * SUBLANES:(g + 1) * SUBLANES, :]
        local.append(b_g if g == 0 else a_g * local[-1] + b_g)
        decay.append(a_g if g == 0 else a_g * decay[-1])
    seg_a, seg_b = decay[-1], local[-1]
    d = 1
    while d < SUBLANES:
        keep = seg_id >= d
        a_prev = jnp.where(keep, pltpu.roll(seg_a, d, axis=0), 1.0)
        b_prev = jnp.where(keep, pltpu.roll(seg_b, d, axis=0), 0.0)
        seg_b = seg_b + seg_a * b_prev
        seg_a = seg_a * a_prev
        d *= 2
    carry = carry_scr[...]
    seg_end = seg_a * carry + seg_b
    seg_in = jnp.where(seg_id == 0, carry, pltpu.roll(seg_end, 1, axis=0))
    carry_scr[...] = seg_end[SUBLANES - 1:SUBLANES, :]
    for g in range(SEG_LEN):
        h_g = local[g] + decay[g] * seg_in
        for slab in range(LRU_SLABS):
            h_scr[slab, pl.ds(g, SUBLANES, stride=SEG_PITCH), :] = h_g[:, slab * LANES:(slab + 1) * LANES]

    h = jnp.concatenate([
        jnp.concatenate([h_scr[slab, seg * SEG_PITCH:seg * SEG_PITCH + SEG_LEN, :] for slab in range(LRU_SLABS)],
                        axis=-1)
        for seg in range(SUBLANES)], axis=0)
    gate_a = proj_scr[:, _P_GATE_A:_P_GATE_A + LRU_WIDTH]
    y_scr[:, 0:LRU_WIDTH] = _gated(h * _row_rms_scale(h), gate_a)

    cos = cos_ref[...]
    sin = sin_ref[...]
    q_groups = []
    for g in range(SWA_WIDTH // LANES):
        cols = slice(_P_Q + g * LANES, _P_Q + (g + 1) * LANES)
        qraw = proj_scr[:, cols]
        qn = qraw * _head_rms_scale(qraw, head_mean_ref) * q_gain
        q_groups.append(_rope(qn, cos, sin))
    kraw = proj_scr[:, _P_K:_P_K + SWA_KV_WIDTH]
    kn = kraw * _head_rms_scale(kraw, head_mean_ref) * k_gain
    for name_scr, kv in ((k_scr, _rope(kn, cos, sin)), (v_scr, proj_scr[:, _P_V:_P_V + SWA_KV_WIDTH])):
        name_scr[0, BLOCK:BLOCK + tile, :] = kv.astype(_BF16)
        name_scr[1, BLOCK:BLOCK + tile, :] = pltpu.roll(kv, HEAD_DIM, axis=1).astype(_BF16)

    xq_groups = []
    for g in range(XATTN_WIDTH // LANES):
        cols = slice(_P_XQ + g * LANES, _P_XQ + (g + 1) * LANES)
        xraw = proj_scr[:, cols]
        xq_groups.append(xraw * _head_rms_scale(xraw, head_mean_ref) * xq_gain)

    lane = lax.broadcasted_iota(jnp.int32, (BLOCK, LANES), 1)
    low_half = lane < HEAD_DIM
    half_masks = (low_half, lane >= HEAD_DIM)
    qi = lax.broadcasted_iota(jnp.int32, (BLOCK, BLOCK), 0)
    kj = lax.broadcasted_iota(jnp.int32, (BLOCK, BLOCK), 1)
    own_block = kj <= qi

    for blk in range(n_blocks):
        rows = slice(blk * BLOCK, (blk + 1) * BLOCK)
        o_swa = [None, None]
        for head, (g, hi) in enumerate(_HEAD_SLOTS):
            kv_copy = int(hi != head // (SWA_Q_HEADS // SWA_KV_HEADS))
            kk = k_scr[kv_copy, blk * BLOCK:(blk + 2) * BLOCK, :]
            vv = v_scr[kv_copy, blk * BLOCK:(blk + 2) * BLOCK, :]
            qm = jnp.where(half_masks[hi], q_groups[g][rows, :], 0.0).astype(_BF16)
            sc_both = _dot_nt(qm, kk)
            sc_prev = sc_both[:, :BLOCK]
            if blk == 0:
                sc_prev = jnp.where(s > 0, sc_prev, NEG_INF)
            sc = jnp.where(own_block, sc_both[:, BLOCK:], sc_prev)
            sink = sinks_ref[layer, head] * LOG2_E
            m = jnp.maximum(jnp.max(sc, axis=-1, keepdims=True), sink)
            p = jnp.exp2(sc - m)
            den = jnp.sum(p, axis=-1, keepdims=True) + jnp.exp2(sink - m)
            p_both = jnp.concatenate([jnp.where(own_block, 0.0, p), jnp.where(own_block, p, 0.0)], axis=-1)
            o = _dot(p_both.astype(_BF16), vv) * (1.0 / den)
            o_swa[g] = o if o_swa[g] is None else jnp.where(low_half, o_swa[g], o)
        y_b = jnp.concatenate(o_swa, axis=-1)
        gate_b = proj_scr[rows, _P_GATE_B:_P_GATE_B + SWA_WIDTH]
        y_scr[rows, LRU_WIDTH:LRU_WIDTH + SWA_WIDTH] = _gated(y_b * _row_rms_scale(y_b), gate_b)

        o_mem = [None, None]
        for head, (g, hi) in enumerate(_HEAD_SLOTS):
            qm = jnp.where(half_masks[hi], xq_groups[g][rows, :], 0.0).astype(_BF16)
            sc = _dot_nt(qm, km_scr[:, g * LANES:(g + 1) * LANES])
            m = jnp.max(sc, axis=-1, keepdims=True)
            p = jnp.exp2(sc - m)
            den = jnp.sum(p, axis=-1, keepdims=True)
            o = _dot(p.astype(_BF16), vm_scr[:, g * LANES:(g + 1) * LANES]) * (1.0 / den)
            o_mem[g] = o if o_mem[g] is None else jnp.where(low_half, o_mem[g], o)
        y_c = jnp.concatenate(o_mem, axis=-1)
        gate_c = proj_scr[rows, _P_GATE_C:_P_GATE_C + XATTN_WIDTH]
        y_scr[rows, LRU_WIDTH + SWA_WIDTH:] = _gated(y_c * _row_rms_scale(y_c), gate_c)

    o_ref[0] = x + _dot(y_scr[...], w_out_ref[...])


def _rope_tables(seq):
    pos = np.arange(seq, dtype=np.float32)
    inv_freq = np.float32(ROPE_THETA) ** (-(np.arange(0, ROPE_DIM, 2, dtype=np.float32) / np.float32(ROPE_DIM)))
    ang = (pos[:, None] * inv_freq[None, :].astype(np.float32)).astype(np.float32)
    cos, sin = np.cos(ang).astype(np.float32), np.sin(ang).astype(np.float32)
    ones = np.ones((seq, HEAD_DIM - ROPE_DIM), np.float32)
    cos_head = np.concatenate([cos, cos, ones], axis=-1)
    sin_head = np.concatenate([-sin, sin, 0.0 * ones], axis=-1)
    reps = LANES // HEAD_DIM
    return jnp.asarray(np.tile(cos_head, (1, reps))), jnp.asarray(np.tile(sin_head, (1, reps)))


def _head_mean_matrix(width):
    block = np.full((HEAD_DIM, HEAD_DIM), 1.0 / HEAD_DIM, np.float32)
    return jnp.asarray(np.kron(np.eye(width // HEAD_DIM, dtype=np.float32), block), dtype=_BF16)


def _const_spec(shape):
    return pl.BlockSpec(shape, lambda *_: (0,) * len(shape))


def _gate_half_columns():
    scale = np.ones((LRU_WIDTH + _P_END,), np.float32)
    for start, width in ((_P_GATE_A, LRU_WIDTH), (_P_GATE_B, SWA_WIDTH), (_P_GATE_C, XATTN_WIDTH)):
        scale[LRU_WIDTH + start:LRU_WIDTH + start + width] = 0.5
    return scale


def _scale_gate_columns(w, first_col):
    half = _gate_half_columns()
    pieces = []
    for c in range(0, w.shape[1], LANES):
        scale = half[first_col + c:first_col + c + LANES]
        assert (scale == scale[0]).all()
        piece = w[:, c:c + LANES]
        pieces.append(piece if scale[0] == 1.0 else piece * float(scale[0]))
    return jnp.concatenate(pieces, axis=-1)


def _layer(x, mem, p, layer, *, tile):
    B, S, _ = x.shape
    n_tiles = S // tile
    assert tile >= D_MODEL and tile >= D_MIX and 2 * XATTN_WIDTH == LRU_WIDTH
    cos, sin = _rope_tables(S)
    vector_names = ("mem_norm_g", "xk_norm_g", "norm_g", "conv_w", "conv_b", "b_rg", "b_ig",
                    "lru_lambda", "q_norm_g", "k_norm_g", "xq_norm_g", "out_norm_g", "w_rg", "w_ig")
    vectors = [p[name].astype(_F32) for name in vector_names]
    big_weights = [p[name].astype(_F32) for name in ("w_mem_kv", "w_in", "w_out")]
    head_mean = _head_mean_matrix(LANES)
    operands = [p["sinks"].astype(_F32), x, mem, cos, sin] + vectors + [head_mean] + big_weights
    layer_block = lambda a: pl.BlockSpec((1,) + a.shape[1:], lambda b, s: (layer,) + (0,) * (a.ndim - 1))
    in_specs = [
        pl.BlockSpec(memory_space=pltpu.SMEM),
        pl.BlockSpec((1, tile, D_MODEL), lambda b, s: (b, s, 0)),
        pl.BlockSpec((1, MEM_LEN, D_MODEL), lambda b, s: (b, 0, 0)),
        pl.BlockSpec((tile, LANES), lambda b, s: (s, 0)), pl.BlockSpec((tile, LANES), lambda b, s: (s, 0)),
    ] + [layer_block(v) for v in vectors] + [_const_spec(head_mean.shape)] + [
        pl.BlockSpec(memory_space=pl.ANY) for _ in big_weights]

    return pl.pallas_call(
        functools.partial(_layer_kernel, tile=tile, layer=layer),
        out_shape=jax.ShapeDtypeStruct(x.shape, x.dtype),
        grid=(B, n_tiles),
        in_specs=in_specs,
        out_specs=pl.BlockSpec((1, tile, D_MODEL), lambda b, s: (b, s, 0)),
        scratch_shapes=[
            pltpu.VMEM((tile, _P_END), _F32),
            pltpu.VMEM((LRU_SLABS, SUBLANES * SEG_PITCH, LANES), _F32),
            pltpu.VMEM((LRU_SLABS, SUBLANES * SEG_PITCH, LANES), _F32),
            pltpu.VMEM((1, LRU_WIDTH), _F32),
            pltpu.VMEM(((CONV_WIDTH - 1) * SUBLANES, LRU_WIDTH), _F32),
            pltpu.VMEM((SWA_KV_HEADS, tile + BLOCK, LANES), _BF16),
            pltpu.VMEM((SWA_KV_HEADS, tile + BLOCK, LANES), _BF16),
            pltpu.VMEM((tile, D_MIX), _BF16),
            pltpu.VMEM((MEM_LEN, XATTN_WIDTH), _BF16),
            pltpu.VMEM((MEM_LEN, XATTN_WIDTH), _BF16),
            pltpu.VMEM((D_MODEL, 2 * XATTN_WIDTH), _BF16),
            pltpu.VMEM((D_MODEL, LRU_WIDTH + _P_END), _BF16),
            pltpu.VMEM((LRU_WIDTH // MXU_DIM, MXU_DIM, 2 * MXU_DIM), _BF16),
            pltpu.VMEM((D_MIX, D_MODEL), _BF16),
            pltpu.SemaphoreType.DMA((2 * LRU_SLABS + 2,)),
        ],
        compiler_params=pltpu.CompilerParams(
            dimension_semantics=("arbitrary", "arbitrary"), vmem_limit_bytes=VMEM_LIMIT_BYTES),
        name="hymba_layer",
    )(*operands)


def kernel(x, mem, norm_g, mem_norm_g, w_in, conv_w, conv_b, w_rg, b_rg, w_ig, b_ig, lru_lambda, q_norm_g,
           k_norm_g, sinks, w_mem_kv, xq_norm_g, xk_norm_g, out_norm_g, w_out):
    depth = w_in.shape[0]
    tile = TILE
    assert x.shape[1] % tile == 0 and tile % BLOCK == 0 and WINDOW == BLOCK
    assert SWA_KV_HEADS == LANES // HEAD_DIM == 2
    params = dict(
        norm_g=norm_g, w_in=w_in, conv_w=conv_w, conv_b=conv_b, w_rg=w_rg, b_rg=b_rg, w_ig=w_ig, b_ig=b_ig,
        lru_lambda=lru_lambda, q_norm_g=q_norm_g, k_norm_g=k_norm_g, sinks=sinks, xq_norm_g=xq_norm_g,
        out_norm_g=out_norm_g, w_out=w_out, mem_norm_g=mem_norm_g, w_mem_kv=w_mem_kv, xk_norm_g=xk_norm_g)
    h = x
    for l in range(depth):
        h = _layer(h, mem, params, l, tile=tile)
    return h
```

```python
import functools
import math

import jax
import jax.numpy as jnp
import numpy as np
from jax import lax
from jax.experimental import pallas as pl
from jax.experimental.pallas import tpu as pltpu

D_MODEL = 1024
MEM_LEN = 256
HEAD_DIM = 64
LRU_WIDTH = 512
LRU_BLOCKS = 8
LRU_BLOCK = LRU_WIDTH // LRU_BLOCKS
CONV_WIDTH = 4
LRU_C = 8.0
SWA_Q_HEADS = 4
SWA_KV_HEADS = 2
SWA_WIDTH = SWA_Q_HEADS * HEAD_DIM
SWA_KV_WIDTH = SWA_KV_HEADS * HEAD_DIM
WINDOW = 128
BLOCK = 128
XATTN_HEADS = 4
XATTN_WIDTH = XATTN_HEADS * HEAD_DIM
D_MIX = LRU_WIDTH + SWA_WIDTH + XATTN_WIDTH
ROPE_THETA = 500000.0
ROPE_DIM = HEAD_DIM // 4
EPS = 1e-6
NEG_INF = -1e30
LOG2_E = math.log2(math.e)

LANES = 128
SUBLANES = 8
MXU_DIM = 256
VMEM_LIMIT_BYTES = 56 * 1024 * 1024

TILE = 1024
SEG_LEN = TILE // SUBLANES
SEG_PITCH = SEG_LEN + SUBLANES
LRU_SLABS = LRU_WIDTH // LANES
SQRT_CLAMP = float(np.finfo(np.float32).tiny)
W_IN_CHUNK = 256
N_WEIGHT_COPIES = LRU_SLABS + (D_MODEL // W_IN_CHUNK) * (LRU_SLABS + 1) + 1

_P_GATE_A = 0
_P_Q = _P_GATE_A + LRU_WIDTH
_P_K = _P_Q + SWA_WIDTH
_P_V = _P_K + SWA_KV_WIDTH
_P_GATE_B = _P_V + SWA_KV_WIDTH
_P_XQ = _P_GATE_B + SWA_WIDTH
_P_GATE_C = _P_XQ + XATTN_WIDTH
_P_END = _P_GATE_C + XATTN_WIDTH

_HEAD_SLOTS = ((0, 0), (0, 1), (1, 0), (1, 1))
_HEAD_PAIRS = ((0, 3), (1, 2))

_BF16 = jnp.bfloat16
_F32 = jnp.float32


def _dot(a, b):
    return jnp.dot(a, b, preferred_element_type=_F32)


def _dot_nt(a, b):
    return lax.dot_general(a, b, (((1,), (1,)), ((), ())), preferred_element_type=_F32)


def _sigmoid_of_half(hz):
    return 0.5 * jnp.tanh(hz) + 0.5


def _gated(v, hz):
    hz = hz.astype(_BF16)
    return v.astype(_BF16) * (hz + hz * jnp.tanh(hz))


def _sqrt_nonneg(v):
    return v * lax.rsqrt(jnp.maximum(v, SQRT_CLAMP))


def _row_rms_scale(v):
    return lax.rsqrt(jnp.mean(v * v, axis=-1, keepdims=True) + EPS)


def _head_rms_scale(v, head_mean_ref):
    sq = (v * v).astype(_BF16)
    return lax.rsqrt(_dot(sq, head_mean_ref[...]) + EPS)


def _rope(v, cos, sin):
    half = ROPE_DIM // 2
    lane = lax.broadcasted_iota(jnp.int32, v.shape, 1) % HEAD_DIM
    partner = jnp.where(lane < half, pltpu.roll(v, LANES - half, axis=1), pltpu.roll(v, half, axis=1))
    return v * cos + partner * sin


def _layer_kernel(
    sinks_ref, x_ref, mem_ref, cos_ref, sin_ref, mem_g_ref, xkg_ref, norm_g_ref, conv_w3_ref, conv_b_ref,
    b_rg_ref, b_ig_ref, lam_ref, qg_ref, kg_ref, xqg_ref, out_g_ref, w_rg_ref, w_ig_ref, head_mean_ref,
    w_mem_hbm, w_in_hbm, w_out_hbm,
    o_ref,
    proj_scr, conv_scr, h_scr, carry_scr, tail_scr, k_scr, v_scr, y_scr, km_scr, vm_scr,
    w_mem_ref, w_in_ref, w_gate_ref, w_out_ref, copy_sems,
    *, tile, layer,
):
    s = pl.program_id(1)
    n_blocks = tile // BLOCK

    first_step = (pl.program_id(0) == 0) & (s == 0)

    def weight_copies():
        n_sems = [0]

        def copy(src, dst):
            n_sems[0] += 1
            return pltpu.make_async_copy(src, dst, copy_sems.at[n_sems[0] - 1])

        rows = pl.ds(0, D_MODEL)
        mem_copies = [copy(w_mem_hbm.at[layer, :, pl.ds(slab * LANES, LANES)], h_scr.at[slab, rows, :])
                      for slab in range(LRU_SLABS)]
        in_chunks = []
        for r0 in range(0, D_MODEL, W_IN_CHUNK):
            rows = pl.ds(r0, W_IN_CHUNK)
            in_chunks.append(
                [copy(w_in_hbm.at[layer, rows, pl.ds(slab * LANES, LANES)], conv_scr.at[slab, rows, :])
                 for slab in range(LRU_SLABS)]
                + [copy(w_in_hbm.at[layer, rows, pl.ds(LRU_WIDTH, _P_END)], proj_scr.at[rows, :])])
        out_copy = copy(w_out_hbm.at[layer], o_ref.at[0, pl.ds(0, D_MIX), :])
        assert n_sems[0] == N_WEIGHT_COPIES
        return mem_copies, in_chunks, out_copy

    def gain_columns(g_ref, r0, width):
        col = jnp.broadcast_to(g_ref[:, r0:r0 + LANES], (LANES, LANES)).T
        return jnp.concatenate([col] * (width // LANES), axis=-1)

    @pl.when(first_step)
    def _prepare_weights():
        mem_copies, in_chunks, out_copy = weight_copies()
        for copy in mem_copies + sum(in_chunks, []) + [out_copy]:
            copy.start()

        spread = (lax.broadcasted_iota(jnp.int32, (LRU_BLOCK, MXU_DIM), 1) % LRU_BLOCK
                  == lax.broadcasted_iota(jnp.int32, (LRU_BLOCK, MXU_DIM), 0)).astype(_BF16)
        on_diagonal = (lax.broadcasted_iota(jnp.int32, (MXU_DIM, MXU_DIM), 0) // LRU_BLOCK
                       == lax.broadcasted_iota(jnp.int32, (MXU_DIM, MXU_DIM), 1) // LRU_BLOCK)
        for part, w_ref in enumerate((w_rg_ref, w_ig_ref)):
            blocks = (0.5 * w_ref[0].reshape(LRU_WIDTH, LRU_BLOCK)).astype(_BF16)
            for j in range(LRU_WIDTH // MXU_DIM):
                wide = _dot(blocks[j * MXU_DIM:(j + 1) * MXU_DIM, :], spread)
                w_gate_ref[j, :, part * MXU_DIM:(part + 1) * MXU_DIM] = jnp.where(on_diagonal, wide, 0.0).astype(_BF16)

        for copy in mem_copies:
            copy.wait()
        for r0 in range(0, D_MODEL, LANES):
            w_rows = jnp.concatenate([h_scr[slab, r0:r0 + LANES, :] for slab in range(LRU_SLABS)], axis=-1)
            w_mem_ref[r0:r0 + LANES, :] = (gain_columns(mem_g_ref, r0, 2 * XATTN_WIDTH) * w_rows).astype(_BF16)
        for r0 in range(0, D_MODEL, LANES):
            if r0 % W_IN_CHUNK == 0:
                for copy in in_chunks[r0 // W_IN_CHUNK]:
                    copy.wait()
            lru_cols = jnp.concatenate([conv_scr[slab, r0:r0 + LANES, :] for slab in range(LRU_SLABS)], axis=-1)
            w_rows = jnp.concatenate([_scale_gate_columns(lru_cols, 0),
                                      _scale_gate_columns(proj_scr[r0:r0 + LANES, :], LRU_WIDTH)], axis=-1)
            w_in_ref[r0:r0 + LANES, :] = (gain_columns(norm_g_ref, r0, LRU_WIDTH + _P_END) * w_rows).astype(_BF16)
        out_copy.wait()
        for r0 in range(0, D_MIX, LANES):
            w_out_ref[r0:r0 + LANES, :] = (gain_columns(out_g_ref, r0, D_MODEL) * o_ref[0, r0:r0 + LANES, :]).astype(_BF16)

    conv_w_ref = conv_w3_ref.at[0]
    score_scale = LOG2_E / math.sqrt(HEAD_DIM)
    two_heads = lambda g_ref: jnp.concatenate([g_ref[...]] * (LANES // HEAD_DIM), axis=1)
    q_gain = two_heads(qg_ref) * score_scale
    k_gain = two_heads(kg_ref)
    xq_gain = two_heads(xqg_ref) * score_scale

    @pl.when(s == 0)
    def _reset_sequence_state():
        tail_scr[...] = jnp.zeros_like(tail_scr)
        carry_scr[...] = jnp.zeros_like(carry_scr)
        k_scr[:, tile:tile + BLOCK, :] = jnp.zeros((SWA_KV_HEADS, BLOCK, LANES), _BF16)
        v_scr[:, tile:tile + BLOCK, :] = jnp.zeros((SWA_KV_HEADS, BLOCK, LANES), _BF16)

    @pl.when(s == 0)
    def _project_memory():
        m = mem_ref[0]
        mn = (m * _row_rms_scale(m)).astype(_BF16)
        kv = _dot(mn, w_mem_ref[...])
        xk_gain = two_heads(xkg_ref) * xq_gain
        for g in range(XATTN_WIDTH // LANES):
            k = kv[:, g * LANES:(g + 1) * LANES]
            km_scr[:, g * LANES:(g + 1) * LANES] = (k * _head_rms_scale(k, head_mean_ref) * xk_gain).astype(_BF16)
        vm_scr[...] = kv[:, XATTN_WIDTH:].astype(_BF16)

    k_scr[:, 0:BLOCK, :] = k_scr[:, tile:tile + BLOCK, :]
    v_scr[:, 0:BLOCK, :] = v_scr[:, tile:tile + BLOCK, :]

    x = x_ref[0]
    xn = (x * _row_rms_scale(x)).astype(_BF16)
    lru_x = _dot(xn, w_in_ref[:, 0:LRU_WIDTH])
    for seg in range(SUBLANES):
        for slab in range(LRU_SLABS):
            conv_scr[slab, seg * SEG_PITCH:seg * SEG_PITCH + SEG_LEN, :] = (
                lru_x[seg * SEG_LEN:(seg + 1) * SEG_LEN, slab * LANES:(slab + 1) * LANES])
    proj_scr[...] = _dot(xn, w_in_ref[:, LRU_WIDTH:])

    seg_id = lax.broadcasted_iota(jnp.int32, (SUBLANES, LRU_WIDTH), 0)

    def strided(ref, g):
        return jnp.concatenate(
            [ref[slab, pl.ds(g, SUBLANES, stride=SEG_PITCH), :] for slab in range(LRU_SLABS)], axis=-1)

    def shift_segments(cur, prev_tile):
        return jnp.where(seg_id == 0, pltpu.roll(prev_tile, 1, axis=0), pltpu.roll(cur, 1, axis=0))

    u = [strided(conv_scr, g) for g in range(SEG_LEN)]
    wrapped = {}
    for k in range(1, CONV_WIDTH):
        wrapped[SEG_LEN - k] = shift_segments(u[SEG_LEN - k], tail_scr[(k - 1) * SUBLANES:k * SUBLANES, :])
    for k in range(1, CONV_WIDTH):
        tail_scr[(k - 1) * SUBLANES:k * SUBLANES, :] = u[SEG_LEN - k]
    xc_parts = []
    for g in range(SEG_LEN):
        xc = conv_b_ref[...] + conv_w_ref[CONV_WIDTH - 1:CONV_WIDTH, :] * u[g]
        for k in range(1, CONV_WIDTH):
            past = u[g - k] if g >= k else wrapped[SEG_LEN + g - k]
            xc = xc + conv_w_ref[CONV_WIDTH - 1 - k:CONV_WIDTH - k, :] * past
        xc_parts.append(xc)
    xc = jnp.concatenate(xc_parts, axis=0)

    lam = -lam_ref[...]
    softplus = jnp.maximum(lam, 0.0) + jnp.log1p(jnp.exp(-jnp.abs(lam)))
    half_c_softplus = 0.5 * LRU_C * softplus
    a_parts, b_parts = [], []
    for j in range(LRU_WIDTH // MXU_DIM):
        cols = slice(j * MXU_DIM, (j + 1) * MXU_DIM)
        pre = _dot(xc[:, cols].astype(_BF16), w_gate_ref[j])
        r_tanh = jnp.tanh(pre[:, :MXU_DIM] + 0.5 * b_rg_ref[:, cols])
        ig = _sigmoid_of_half(pre[:, MXU_DIM:] + 0.5 * b_ig_ref[:, cols])
        neg_log_a = r_tanh * half_c_softplus[:, cols] + half_c_softplus[:, cols]
        a = jnp.exp2(neg_log_a * -LOG2_E)
        a_parts.append(a)
        b_parts.append(_sqrt_nonneg(jnp.tanh(neg_log_a) * (1.0 + a * a)) * (ig * xc[:, cols]))
    a_all = jnp.concatenate(a_parts, axis=-1)
    b_all = jnp.concatenate(b_parts, axis=-1)

    local, decay = [], []
    for g in range(SEG_LEN):
        a_g = a_all[g * SUBLANES:(g + 1) * SUBLANES, :]
        b_g = b_all[g * SUBLANES:(g + 1) * SUBLANES, :]
        local.append(b_g if g == 0 else a_g * local[-1] + b_g)
        decay.append(a_g if g == 0 else a_g * decay[-1])
    seg_a, seg_b = decay[-1], local[-1]
    d = 1
    while d < SUBLANES:
        keep = seg_id >= d
        a_prev = jnp.where(keep, pltpu.roll(seg_a, d, axis=0), 1.0)
        b_prev = jnp.where(keep, pltpu.roll(seg_b, d, axis=0), 0.0)
        seg_b = seg_b + seg_a * b_prev
        seg_a = seg_a * a_prev
        d *= 2
    carry = carry_scr[...]
    seg_end = seg_a * carry + seg_b
    seg_in = jnp.where(seg_id == 0, carry, pltpu.roll(seg_end, 1, axis=0))
    carry_scr[...] = seg_end[SUBLANES - 1:SUBLANES, :]
    for g in range(SEG_LEN):
        h_g = local[g] + decay[g] * seg_in
        for slab in range(LRU_SLABS):
            h_scr[slab, pl.ds(g, SUBLANES, stride=SEG_PITCH), :] = h_g[:, slab * LANES:(slab + 1) * LANES]

    h = jnp.concatenate([
        jnp.concatenate([h_scr[slab, seg * SEG_PITCH:seg * SEG_PITCH + SEG_LEN, :] for slab in range(LRU_SLABS)],
                        axis=-1)
        for seg in range(SUBLANES)], axis=0)
    gate_a = proj_scr[:, _P_GATE_A:_P_GATE_A + LRU_WIDTH]
    y_scr[:, 0:LRU_WIDTH] = _gated(h * _row_rms_scale(h), gate_a)

    cos = cos_ref[...]
    sin = sin_ref[...]
    q_groups = []
    for g in range(SWA_WIDTH // LANES):
        cols = slice(_P_Q + g * LANES, _P_Q + (g + 1) * LANES)
        qraw = proj_scr[:, cols]
        qn = qraw * _head_rms_scale(qraw, head_mean_ref) * q_gain
        q_groups.append(_rope(qn, cos, sin))
    kraw = proj_scr[:, _P_K:_P_K + SWA_KV_WIDTH]
    kn = kraw * _head_rms_scale(kraw, head_mean_ref) * k_gain
    for name_scr, kv in ((k_scr, _rope(kn, cos, sin)), (v_scr, proj_scr[:, _P_V:_P_V + SWA_KV_WIDTH])):
        name_scr[0, BLOCK:BLOCK + tile, :] = kv.astype(_BF16)
        name_scr[1, BLOCK:BLOCK + tile, :] = pltpu.roll(kv, HEAD_DIM, axis=1).astype(_BF16)

    xq_groups = []
    for g in range(XATTN_WIDTH // LANES):
        cols = slice(_P_XQ + g * LANES, _P_XQ + (g + 1) * LANES)
        xraw = proj_scr[:, cols]
        xq_groups.append(xraw * _head_rms_scale(xraw, head_mean_ref))

    lane = lax.broadcasted_iota(jnp.int32, (BLOCK, LANES), 1)
    low_half = lane < HEAD_DIM
    half_masks = (low_half, lane >= HEAD_DIM)
    qi = lax.broadcasted_iota(jnp.int32, (BLOCK, BLOCK), 0)
    kj = lax.broadcasted_iota(jnp.int32, (BLOCK, BLOCK), 1)
    own_block = kj <= qi

    for blk in range(n_blocks):
        rows = slice(blk * BLOCK, (blk + 1) * BLOCK)
        o_head = [None] * SWA_Q_HEADS
        for kv_copy, heads in enumerate(_HEAD_PAIRS):
            kk = k_scr[kv_copy, blk * BLOCK:(blk + 2) * BLOCK, :]
            vv = v_scr[kv_copy, blk * BLOCK:(blk + 2) * BLOCK, :]
            qm = jnp.concatenate([
                jnp.where(half_masks[_HEAD_SLOTS[head][1]], q_groups[_HEAD_SLOTS[head][0]][rows, :], 0.0).astype(_BF16)
                for head in heads], axis=0)
            sc_pair = _dot_nt(qm, kk)
            p_pair, dens = [], []
            for i, head in enumerate(heads):
                sc_both = sc_pair[i * BLOCK:(i + 1) * BLOCK, :]
                sc_prev = sc_both[:, :BLOCK]
                if blk == 0:
                    sc_prev = jnp.where(s > 0, sc_prev, NEG_INF)
                sc = jnp.where(own_block, sc_both[:, BLOCK:], sc_prev)
                sink = sinks_ref[layer, head] * LOG2_E
                m = jnp.maximum(jnp.max(sc, axis=-1, keepdims=True), sink)
                p = jnp.exp2(sc - m)
                dens.append(jnp.sum(p, axis=-1, keepdims=True) + jnp.exp2(sink - m))
                p_pair.append(jnp.concatenate(
                    [jnp.where(own_block, 0.0, p), jnp.where(own_block, p, 0.0)], axis=-1).astype(_BF16))
            o_pair = _dot(jnp.concatenate(p_pair, axis=0), vv)
            for i, head in enumerate(heads):
                o_head[head] = o_pair[i * BLOCK:(i + 1) * BLOCK, :] * (1.0 / dens[i])
        o_swa = [jnp.where(low_half, o_head[2 * g], o_head[2 * g + 1]) for g in range(SWA_WIDTH // LANES)]
        y_b = jnp.concatenate(o_swa, axis=-1)
        gate_b = proj_scr[rows, _P_GATE_B:_P_GATE_B + SWA_WIDTH]
        y_scr[rows, LRU_WIDTH:LRU_WIDTH + SWA_WIDTH] = _gated(y_b * _row_rms_scale(y_b), gate_b)

        o_mem = []
        for g in range(XATTN_WIDTH // LANES):
            qm = jnp.concatenate([jnp.where(half_masks[hi], xq_groups[g][rows, :], 0.0).astype(_BF16)
                                  for hi in range(LANES // HEAD_DIM)], axis=0)
            sc = _dot_nt(qm, km_scr[:, g * LANES:(g + 1) * LANES])
            m = jnp.max(sc, axis=-1, keepdims=True)
            p = jnp.exp2(sc - m)
            den = jnp.sum(p, axis=-1, keepdims=True)
            o = _dot(p.astype(_BF16), vm_scr[:, g * LANES:(g + 1) * LANES]) * (1.0 / den)
            o_mem.append(jnp.where(low_half, o[:BLOCK, :], o[BLOCK:, :]))
        y_c = jnp.concatenate(o_mem, axis=-1)
        gate_c = proj_scr[rows, _P_GATE_C:_P_GATE_C + XATTN_WIDTH]
        y_scr[rows, LRU_WIDTH + SWA_WIDTH:] = _gated(y_c * _row_rms_scale(y_c), gate_c)

    o_ref[0] = x + _dot(y_scr[...], w_out_ref[...])


def _rope_tables(seq):
    pos = np.arange(seq, dtype=np.float32)
    inv_freq = np.float32(ROPE_THETA) ** (-(np.arange(0, ROPE_DIM, 2, dtype=np.float32) / np.float32(ROPE_DIM)))
    ang = (pos[:, None] * inv_freq[None, :].astype(np.float32)).astype(np.float32)
    cos, sin = np.cos(ang).astype(np.float32), np.sin(ang).astype(np.float32)
    ones = np.ones((seq, HEAD_DIM - ROPE_DIM), np.float32)
    cos_head = np.concatenate([cos, cos, ones], axis=-1)
    sin_head = np.concatenate([-sin, sin, 0.0 * ones], axis=-1)
    reps = LANES // HEAD_DIM
    return jnp.asarray(np.tile(cos_head, (1, reps))), jnp.asarray(np.tile(sin_head, (1, reps)))


def _head_mean_matrix(width):
    block = np.full((HEAD_DIM, HEAD_DIM), 1.0 / HEAD_DIM, np.float32)
    return jnp.asarray(np.kron(np.eye(width // HEAD_DIM, dtype=np.float32), block), dtype=_BF16)


def _const_spec(shape):
    return pl.BlockSpec(shape, lambda *_: (0,) * len(shape))


def _gate_half_columns():
    scale = np.ones((LRU_WIDTH + _P_END,), np.float32)
    for start, width in ((_P_GATE_A, LRU_WIDTH), (_P_GATE_B, SWA_WIDTH), (_P_GATE_C, XATTN_WIDTH)):
        scale[LRU_WIDTH + start:LRU_WIDTH + start + width] = 0.5
    return scale


def _scale_gate_columns(w, first_col):
    half = _gate_half_columns()
    pieces = []
    for c in range(0, w.shape[1], LANES):
        scale = half[first_col + c:first_col + c + LANES]
        assert (scale == scale[0]).all()
        piece = w[:, c:c + LANES]
        pieces.append(piece if scale[0] == 1.0 else piece * float(scale[0]))
    return jnp.concatenate(pieces, axis=-1)


def _layer(x, mem, p, layer, *, tile):
    B, S, _ = x.shape
    n_tiles = S // tile
    assert tile >= D_MODEL and tile >= D_MIX and 2 * XATTN_WIDTH == LRU_WIDTH
    cos, sin = _rope_tables(S)
    vector_names = ("mem_norm_g", "xk_norm_g", "norm_g", "conv_w", "conv_b", "b_rg", "b_ig",
                    "lru_lambda", "q_norm_g", "k_norm_g", "xq_norm_g", "out_norm_g", "w_rg", "w_ig")
    vectors = [p[name].astype(_F32) for name in vector_names]
    big_weights = [p[name].astype(_F32) for name in ("w_mem_kv", "w_in", "w_out")]
    head_mean = _head_mean_matrix(LANES)
    operands = [p["sinks"].astype(_F32), x, mem, cos, sin] + vectors + [head_mean] + big_weights
    layer_block = lambda a: pl.BlockSpec((1,) + a.shape[1:], lambda b, s: (layer,) + (0,) * (a.ndim - 1))
    in_specs = [
        pl.BlockSpec(memory_space=pltpu.SMEM),
        pl.BlockSpec((1, tile, D_MODEL), lambda b, s: (b, s, 0)),
        pl.BlockSpec((1, MEM_LEN, D_MODEL), lambda b, s: (b, 0, 0)),
        pl.BlockSpec((tile, LANES), lambda b, s: (s, 0)), pl.BlockSpec((tile, LANES), lambda b, s: (s, 0)),
    ] + [layer_block(v) for v in vectors] + [_const_spec(head_mean.shape)] + [
        pl.BlockSpec(memory_space=pl.ANY) for _ in big_weights]

    return pl.pallas_call(
        functools.partial(_layer_kernel, tile=tile, layer=layer),
        out_shape=jax.ShapeDtypeStruct(x.shape, x.dtype),
        grid=(B, n_tiles),
        in_specs=in_specs,
        out_specs=pl.BlockSpec((1, tile, D_MODEL), lambda b, s: (b, s, 0)),
        scratch_shapes=[
            pltpu.VMEM((tile, _P_END), _F32),
            pltpu.VMEM((LRU_SLABS, SUBLANES * SEG_PITCH, LANES), _F32),
            pltpu.VMEM((LRU_SLABS, SUBLANES * SEG_PITCH, LANES), _F32),
            pltpu.VMEM((1, LRU_WIDTH), _F32),
            pltpu.VMEM(((CONV_WIDTH - 1) * SUBLANES, LRU_WIDTH), _F32),
            pltpu.VMEM((SWA_KV_HEADS, tile + BLOCK, LANES), _BF16),
            pltpu.VMEM((SWA_KV_HEADS, tile + BLOCK, LANES), _BF16),
            pltpu.VMEM((tile, D_MIX), _BF16),
            pltpu.VMEM((MEM_LEN, XATTN_WIDTH), _BF16),
            pltpu.VMEM((MEM_LEN, XATTN_WIDTH), _BF16),
            pltpu.VMEM((D_MODEL, 2 * XATTN_WIDTH), _BF16),
            pltpu.VMEM((D_MODEL, LRU_WIDTH + _P_END), _BF16),
            pltpu.VMEM((LRU_WIDTH // MXU_DIM, MXU_DIM, 2 * MXU_DIM), _BF16),
            pltpu.VMEM((D_MIX, D_MODEL), _BF16),
            pltpu.SemaphoreType.DMA((N_WEIGHT_COPIES,)),
        ],
        compiler_params=pltpu.CompilerParams(
            dimension_semantics=("arbitrary", "arbitrary"), vmem_limit_bytes=VMEM_LIMIT_BYTES),
        name="hymba_layer",
    )(*operands)


def kernel(x, mem, norm_g, mem_norm_g, w_in, conv_w, conv_b, w_rg, b_rg, w_ig, b_ig, lru_lambda, q_norm_g,
           k_norm_g, sinks, w_mem_kv, xq_norm_g, xk_norm_g, out_norm_g, w_out):
    depth = w_in.shape[0]
    tile = TILE
    assert x.shape[1] % tile == 0 and tile % BLOCK == 0 and WINDOW == BLOCK
    assert SWA_KV_HEADS == LANES // HEAD_DIM == 2
    params = dict(
        norm_g=norm_g, w_in=w_in, conv_w=conv_w, conv_b=conv_b, w_rg=w_rg, b_rg=b_rg, w_ig=w_ig, b_ig=b_ig,
        lru_lambda=lru_lambda, q_norm_g=q_norm_g, k_norm_g=k_norm_g, sinks=sinks, xq_norm_g=xq_norm_g,
        out_norm_g=out_norm_g, w_out=w_out, mem_norm_g=mem_norm_g, w_mem_kv=w_mem_kv, xk_norm_g=xk_norm_g)
    h = x
    for l in range(depth):
        h = _layer(h, mem, params, l, tile=tile)
    return h
```

```python
import functools
import math

import jax
import jax.numpy as jnp
import numpy as np
from jax import lax
from jax.experimental import pallas as pl
from jax.experimental.pallas import tpu as pltpu

D_MODEL = 1024
MEM_LEN = 256
HEAD_DIM = 64
LRU_WIDTH = 512
LRU_BLOCKS = 8
LRU_BLOCK = LRU_WIDTH // LRU_BLOCKS
CONV_WIDTH = 4
LRU_C = 8.0
SWA_Q_HEADS = 4
SWA_KV_HEADS = 2
SWA_WIDTH = SWA_Q_HEADS * HEAD_DIM
SWA_KV_WIDTH = SWA_KV_HEADS * HEAD_DIM
WINDOW = 128
BLOCK = 128
XATTN_HEADS = 4
XATTN_WIDTH = XATTN_HEADS * HEAD_DIM
D_MIX = LRU_WIDTH + SWA_WIDTH + XATTN_WIDTH
ROPE_THETA = 500000.0
ROPE_DIM = HEAD_DIM // 4
EPS = 1e-6
NEG_INF = -1e30
LOG2_E = math.log2(math.e)

LANES = 128
SUBLANES = 8
MXU_DIM = 256
VMEM_LIMIT_BYTES = 56 * 1024 * 1024

TILE = 1024
SEG_LEN = TILE // SUBLANES
SEG_PITCH = SEG_LEN + SUBLANES
LRU_SLABS = LRU_WIDTH // LANES
SQRT_CLAMP = float(np.finfo(np.float32).tiny)
W_IN_CHUNK = 256
N_WEIGHT_COPIES = LRU_SLABS + (D_MODEL // W_IN_CHUNK) * (LRU_SLABS + 1) + 1

_P_GATE_A = 0
_P_Q = _P_GATE_A + LRU_WIDTH
_P_K = _P_Q + SWA_WIDTH
_P_V = _P_K + SWA_KV_WIDTH
_P_GATE_B = _P_V + SWA_KV_WIDTH
_P_XQ = _P_GATE_B + SWA_WIDTH
_P_GATE_C = _P_XQ + XATTN_WIDTH
_P_END = _P_GATE_C + XATTN_WIDTH

_HEAD_SLOTS = ((0, 0), (1, 0), (0, 1), (1, 1))

_BF16 = jnp.bfloat16
_F32 = jnp.float32


def _dot(a, b):
    return jnp.dot(a, b, preferred_element_type=_F32)


def _dot_nt(a, b):
    return lax.dot_general(a, b, (((1,), (1,)), ((), ())), preferred_element_type=_F32)


def _sigmoid_of_half(hz):
    return 0.5 * jnp.tanh(hz) + 0.5


def _gated(v, hz):
    hz = hz.astype(_BF16)
    return v.astype(_BF16) * (hz + hz * jnp.tanh(hz))


def _sqrt_nonneg(v):
    return v * lax.rsqrt(jnp.maximum(v, SQRT_CLAMP))


def _row_rms_scale(v):
    return lax.rsqrt(jnp.mean(v * v, axis=-1, keepdims=True) + EPS)


def _head_rms_scale(v, head_mean_ref):
    sq = (v * v).astype(_BF16)
    return lax.rsqrt(_dot(sq, head_mean_ref[...]) + EPS)


def _rope(v, cos, sin):
    half = ROPE_DIM // 2
    lane = lax.broadcasted_iota(jnp.int32, v.shape, 1) % HEAD_DIM
    partner = jnp.where(lane < half, pltpu.roll(v, LANES - half, axis=1), pltpu.roll(v, half, axis=1))
    return v * cos + partner * sin


def _layer_kernel(
    sinks_ref, x_ref, mem_ref, cos_ref, sin_ref, mem_g_ref, xkg_ref, norm_g_ref, conv_w3_ref, conv_b_ref,
    b_rg_ref, b_ig_ref, lam_ref, qg_ref, kg_ref, xqg_ref, out_g_ref, w_rg_ref, w_ig_ref, head_mean_ref,
    w_mem_hbm, w_in_hbm, w_out_hbm,
    o_ref,
    proj_scr, conv_scr, h_scr, carry_scr, tail_scr, k_scr, v_scr, y_scr, km_scr, vm_scr,
    w_mem_ref, w_in_ref, w_gate_ref, w_out_ref, copy_sems,
    *, tile, layer,
):
    s = pl.program_id(1)
    n_blocks = tile // BLOCK

    first_step = (pl.program_id(0) == 0) & (s == 0)

    def weight_copies():
        n_sems = [0]

        def copy(src, dst):
            n_sems[0] += 1
            return pltpu.make_async_copy(src, dst, copy_sems.at[n_sems[0] - 1])

        rows = pl.ds(0, D_MODEL)
        mem_copies = [copy(w_mem_hbm.at[layer, :, pl.ds(slab * LANES, LANES)], h_scr.at[slab, rows, :])
                      for slab in range(LRU_SLABS)]
        in_chunks = []
        for r0 in range(0, D_MODEL, W_IN_CHUNK):
            rows = pl.ds(r0, W_IN_CHUNK)
            in_chunks.append(
                [copy(w_in_hbm.at[layer, rows, pl.ds(slab * LANES, LANES)], conv_scr.at[slab, rows, :])
                 for slab in range(LRU_SLABS)]
                + [copy(w_in_hbm.at[layer, rows, pl.ds(LRU_WIDTH, _P_END)], proj_scr.at[rows, :])])
        out_copy = copy(w_out_hbm.at[layer], o_ref.at[0, pl.ds(0, D_MIX), :])
        assert n_sems[0] == N_WEIGHT_COPIES
        return mem_copies, in_chunks, out_copy

    def gain_columns(g_ref, r0, width):
        col = jnp.broadcast_to(g_ref[:, r0:r0 + LANES], (LANES, LANES)).T
        return jnp.concatenate([col] * (width // LANES), axis=-1)

    @pl.when(first_step)
    def _prepare_weights():
        mem_copies, in_chunks, out_copy = weight_copies()
        for copy in mem_copies + sum(in_chunks, []) + [out_copy]:
            copy.start()

        spread = (lax.broadcasted_iota(jnp.int32, (LRU_BLOCK, MXU_DIM), 1) % LRU_BLOCK
                  == lax.broadcasted_iota(jnp.int32, (LRU_BLOCK, MXU_DIM), 0)).astype(_BF16)
        on_diagonal = (lax.broadcasted_iota(jnp.int32, (MXU_DIM, MXU_DIM), 0) // LRU_BLOCK
                       == lax.broadcasted_iota(jnp.int32, (MXU_DIM, MXU_DIM), 1) // LRU_BLOCK)
        for part, w_ref in enumerate((w_rg_ref, w_ig_ref)):
            blocks = (0.5 * w_ref[0].reshape(LRU_WIDTH, LRU_BLOCK)).astype(_BF16)
            for j in range(LRU_WIDTH // MXU_DIM):
                wide = _dot(blocks[j * MXU_DIM:(j + 1) * MXU_DIM, :], spread)
                w_gate_ref[j, :, part * MXU_DIM:(part + 1) * MXU_DIM] = jnp.where(on_diagonal, wide, 0.0).astype(_BF16)

        for copy in mem_copies:
            copy.wait()
        for r0 in range(0, D_MODEL, LANES):
            w_rows = jnp.concatenate([h_scr[slab, r0:r0 + LANES, :] for slab in range(LRU_SLABS)], axis=-1)
            w_mem_ref[r0:r0 + LANES, :] = (gain_columns(mem_g_ref, r0, 2 * XATTN_WIDTH) * w_rows).astype(_BF16)
        for r0 in range(0, D_MODEL, LANES):
            if r0 % W_IN_CHUNK == 0:
                for copy in in_chunks[r0 // W_IN_CHUNK]:
                    copy.wait()
            lru_cols = jnp.concatenate([conv_scr[slab, r0:r0 + LANES, :] for slab in range(LRU_SLABS)], axis=-1)
            rest = _scale_gate_columns(proj_scr[r0:r0 + LANES, :], LRU_WIDTH)
            pieces = [rest[:, c:c + LANES] for c in range(0, _P_END, LANES)]
            low = lax.broadcasted_iota(jnp.int32, (LANES, LANES), 1) < HEAD_DIM
            for first in (_P_Q // LANES, _P_GATE_B // LANES):
                g0, g1 = pieces[first], pieces[first + 1]
                pieces[first] = jnp.where(low, g0, pltpu.roll(g1, HEAD_DIM, axis=1))
                pieces[first + 1] = jnp.where(low, pltpu.roll(g0, HEAD_DIM, axis=1), g1)
            w_rows = jnp.concatenate([_scale_gate_columns(lru_cols, 0)] + pieces, axis=-1)
            w_in_ref[r0:r0 + LANES, :] = (gain_columns(norm_g_ref, r0, LRU_WIDTH + _P_END) * w_rows).astype(_BF16)
        out_copy.wait()
        for r0 in range(0, D_MIX, LANES):
            w_rows = (gain_columns(out_g_ref, r0, D_MODEL) * o_ref[0, r0:r0 + LANES, :]).astype(_BF16)
            if LRU_WIDTH <= r0 < LRU_WIDTH + SWA_WIDTH:
                for half in range(LANES // HEAD_DIM):
                    head = (r0 - LRU_WIDTH) // HEAD_DIM + half
                    slot = LRU_WIDTH + (_HEAD_SLOTS[head][0] * (LANES // HEAD_DIM) + _HEAD_SLOTS[head][1]) * HEAD_DIM
                    w_out_ref[slot:slot + HEAD_DIM, :] = w_rows[half * HEAD_DIM:(half + 1) * HEAD_DIM, :]
            else:
                w_out_ref[r0:r0 + LANES, :] = w_rows

    conv_w_ref = conv_w3_ref.at[0]
    score_scale = LOG2_E / math.sqrt(HEAD_DIM)
    two_heads = lambda g_ref: jnp.concatenate([g_ref[...]] * (LANES // HEAD_DIM), axis=1)
    q_gain = two_heads(qg_ref) * score_scale
    k_gain = two_heads(kg_ref)
    xq_gain = two_heads(xqg_ref) * score_scale

    @pl.when(s == 0)
    def _reset_sequence_state():
        tail_scr[...] = jnp.zeros_like(tail_scr)
        carry_scr[...] = jnp.zeros_like(carry_scr)
        k_scr[tile:tile + BLOCK, :] = jnp.zeros((BLOCK, LANES), _BF16)
        v_scr[tile:tile + BLOCK, :] = jnp.zeros((BLOCK, LANES), _BF16)

    @pl.when(s == 0)
    def _project_memory():
        m = mem_ref[0]
        mn = (m * _row_rms_scale(m)).astype(_BF16)
        kv = _dot(mn, w_mem_ref[...])
        xk_gain = two_heads(xkg_ref) * xq_gain
        for g in range(XATTN_WIDTH // LANES):
            k = kv[:, g * LANES:(g + 1) * LANES]
            km_scr[:, g * LANES:(g + 1) * LANES] = (k * _head_rms_scale(k, head_mean_ref) * xk_gain).astype(_BF16)
        vm_scr[...] = kv[:, XATTN_WIDTH:].astype(_BF16)

    k_scr[0:BLOCK, :] = k_scr[tile:tile + BLOCK, :]
    v_scr[0:BLOCK, :] = v_scr[tile:tile + BLOCK, :]

    x = x_ref[0]
    xn = (x * _row_rms_scale(x)).astype(_BF16)
    lru_x = _dot(xn, w_in_ref[:, 0:LRU_WIDTH])
    for seg in range(SUBLANES):
        for slab in range(LRU_SLABS):
            conv_scr[slab, seg * SEG_PITCH:seg * SEG_PITCH + SEG_LEN, :] = (
                lru_x[seg * SEG_LEN:(seg + 1) * SEG_LEN, slab * LANES:(slab + 1) * LANES])
    proj_scr[...] = _dot(xn, w_in_ref[:, LRU_WIDTH:])

    seg_id = lax.broadcasted_iota(jnp.int32, (SUBLANES, LRU_WIDTH), 0)

    def strided(ref, g):
        return jnp.concatenate(
            [ref[slab, pl.ds(g, SUBLANES, stride=SEG_PITCH), :] for slab in range(LRU_SLABS)], axis=-1)

    def shift_segments(cur, prev_tile):
        return jnp.where(seg_id == 0, pltpu.roll(prev_tile, 1, axis=0), pltpu.roll(cur, 1, axis=0))

    u = [strided(conv_scr, g) for g in range(SEG_LEN)]
    wrapped = {}
    for k in range(1, CONV_WIDTH):
        wrapped[SEG_LEN - k] = shift_segments(u[SEG_LEN - k], tail_scr[(k - 1) * SUBLANES:k * SUBLANES, :])
    for k in range(1, CONV_WIDTH):
        tail_scr[(k - 1) * SUBLANES:k * SUBLANES, :] = u[SEG_LEN - k]
    xc_parts = []
    for g in range(SEG_LEN):
        xc = conv_b_ref[...] + conv_w_ref[CONV_WIDTH - 1:CONV_WIDTH, :] * u[g]
        for k in range(1, CONV_WIDTH):
            past = u[g - k] if g >= k else wrapped[SEG_LEN + g - k]
            xc = xc + conv_w_ref[CONV_WIDTH - 1 - k:CONV_WIDTH - k, :] * past
        xc_parts.append(xc)
    xc = jnp.concatenate(xc_parts, axis=0)

    lam = -lam_ref[...]
    softplus = jnp.maximum(lam, 0.0) + jnp.log1p(jnp.exp(-jnp.abs(lam)))
    half_c_softplus = 0.5 * LRU_C * softplus
    a_parts, b_parts = [], []
    for j in range(LRU_WIDTH // MXU_DIM):
        cols = slice(j * MXU_DIM, (j + 1) * MXU_DIM)
        pre = _dot(xc[:, cols].astype(_BF16), w_gate_ref[j])
        r_tanh = jnp.tanh(pre[:, :MXU_DIM] + 0.5 * b_rg_ref[:, cols])
        ig = _sigmoid_of_half(pre[:, MXU_DIM:] + 0.5 * b_ig_ref[:, cols])
        neg_log_a = r_tanh * half_c_softplus[:, cols] + half_c_softplus[:, cols]
        a = jnp.exp2(neg_log_a * -LOG2_E)
        a_parts.append(a)
        b_parts.append(_sqrt_nonneg(jnp.tanh(neg_log_a) * (1.0 + a * a)) * (ig * xc[:, cols]))
    a_all = jnp.concatenate(a_parts, axis=-1)
    b_all = jnp.concatenate(b_parts, axis=-1)

    local, decay = [], []
    for g in range(SEG_LEN):
        a_g = a_all[g * SUBLANES:(g + 1) * SUBLANES, :]
        b_g = b_all[g * SUBLANES:(g + 1) * SUBLANES, :]
        local.append(b_g if g == 0 else a_g * local[-1] + b_g)
        decay.append(a_g if g == 0 else a_g * decay[-1])
    seg_a, seg_b = decay[-1], local[-1]
    d = 1
    while d < SUBLANES:
        keep = seg_id >= d
        a_prev = jnp.where(keep, pltpu.roll(seg_a, d, axis=0), 1.0)
        b_prev = jnp.where(keep, pltpu.roll(seg_b, d, axis=0), 0.0)
        seg_b = seg_b + seg_a * b_prev
        seg_a = seg_a * a_prev
        d *= 2
    carry = carry_scr[...]
    seg_end = seg_a * carry + seg_b
    seg_in = jnp.where(seg_id == 0, carry, pltpu.roll(seg_end, 1, axis=0))
    carry_scr[...] = seg_end[SUBLANES - 1:SUBLANES, :]
    for g in range(SEG_LEN):
        h_g = local[g] + decay[g] * seg_in
        for slab in range(LRU_SLABS):
            h_scr[slab, pl.ds(g, SUBLANES, stride=SEG_PITCH), :] = h_g[:, slab * LANES:(slab + 1) * LANES]

    h = jnp.concatenate([
        jnp.concatenate([h_scr[slab, seg * SEG_PITCH:seg * SEG_PITCH + SEG_LEN, :] for slab in range(LRU_SLABS)],
                        axis=-1)
        for seg in range(SUBLANES)], axis=0)
    gate_a = proj_scr[:, _P_GATE_A:_P_GATE_A + LRU_WIDTH]
    y_scr[:, 0:LRU_WIDTH] = _gated(h * _row_rms_scale(h), gate_a)

    cos = cos_ref[...]
    sin = sin_ref[...]
    q_groups = []
    for g in range(SWA_WIDTH // LANES):
        cols = slice(_P_Q + g * LANES, _P_Q + (g + 1) * LANES)
        qraw = proj_scr[:, cols]
        qn = qraw * _head_rms_scale(qraw, head_mean_ref) * q_gain
        q_groups.append(_rope(qn, cos, sin))
    kraw = proj_scr[:, _P_K:_P_K + SWA_KV_WIDTH]
    kn = kraw * _head_rms_scale(kraw, head_mean_ref) * k_gain
    for name_scr, kv in ((k_scr, _rope(kn, cos, sin)), (v_scr, proj_scr[:, _P_V:_P_V + SWA_KV_WIDTH])):
        name_scr[BLOCK:BLOCK + tile, :] = kv.astype(_BF16)

    xq_groups = []
    for g in range(XATTN_WIDTH // LANES):
        cols = slice(_P_XQ + g * LANES, _P_XQ + (g + 1) * LANES)
        xraw = proj_scr[:, cols]
        xq_groups.append(xraw * _head_rms_scale(xraw, head_mean_ref))

    lane = lax.broadcasted_iota(jnp.int32, (BLOCK, LANES), 1)
    low_half = lane < HEAD_DIM
    half_masks = (low_half, lane >= HEAD_DIM)
    qi = lax.broadcasted_iota(jnp.int32, (BLOCK, BLOCK), 0)
    kj = lax.broadcasted_iota(jnp.int32, (BLOCK, BLOCK), 1)
    own_block = kj <= qi

    for blk in range(n_blocks):
        rows = slice(blk * BLOCK, (blk + 1) * BLOCK)
        o_head = [None] * SWA_Q_HEADS
        for heads in (tuple(range(SWA_Q_HEADS)),):
            kk = k_scr[blk * BLOCK:(blk + 2) * BLOCK, :]
            vv = v_scr[blk * BLOCK:(blk + 2) * BLOCK, :]
            qm = jnp.concatenate([
                jnp.where(half_masks[_HEAD_SLOTS[head][1]], q_groups[_HEAD_SLOTS[head][0]][rows, :], 0.0).astype(_BF16)
                for head in heads], axis=0)
            sc_pair = _dot_nt(qm, kk)
            p_pair, dens = [], []
            for i, head in enumerate(heads):
                sc_both = sc_pair[i * BLOCK:(i + 1) * BLOCK, :]
                sc_prev = sc_both[:, :BLOCK]
                if blk == 0:
                    sc_prev = jnp.where(s > 0, sc_prev, NEG_INF)
                sc = jnp.where(own_block, sc_both[:, BLOCK:], sc_prev)
                sink = sinks_ref[layer, head] * LOG2_E
                m = jnp.maximum(jnp.max(sc, axis=-1, keepdims=True), sink)
                p = jnp.exp2(sc - m)
                dens.append(jnp.sum(p, axis=-1, keepdims=True) + jnp.exp2(sink - m))
                p_pair.append(jnp.concatenate(
                    [jnp.where(own_block, 0.0, p), jnp.where(own_block, p, 0.0)], axis=-1).astype(_BF16))
            o_pair = _dot(jnp.concatenate(p_pair, axis=0), vv)
            for i, head in enumerate(heads):
                o_head[head] = o_pair[i * BLOCK:(i + 1) * BLOCK, :] * (1.0 / dens[i])
        o_swa = [jnp.where(low_half, o_head[_HEAD_SLOTS.index((g, 0))], o_head[_HEAD_SLOTS.index((g, 1))])
                 for g in range(SWA_WIDTH // LANES)]
        y_b = jnp.concatenate(o_swa, axis=-1)
        gate_b = proj_scr[rows, _P_GATE_B:_P_GATE_B + SWA_WIDTH]
        y_scr[rows, LRU_WIDTH:LRU_WIDTH + SWA_WIDTH] = _gated(y_b * _row_rms_scale(y_b), gate_b)

        o_mem = []
        for g in range(XATTN_WIDTH // LANES):
            qm = jnp.concatenate([jnp.where(half_masks[hi], xq_groups[g][rows, :], 0.0).astype(_BF16)
                                  for hi in range(LANES // HEAD_DIM)], axis=0)
            sc = _dot_nt(qm, km_scr[:, g * LANES:(g + 1) * LANES])
            m = jnp.max(sc, axis=-1, keepdims=True)
            p = jnp.exp2(sc - m)
            den = jnp.sum(p, axis=-1, keepdims=True)
            o = _dot(p.astype(_BF16), vm_scr[:, g * LANES:(g + 1) * LANES]) * (1.0 / den)
            o_mem.append(jnp.where(low_half, o[:BLOCK, :], o[BLOCK:, :]))
        y_c = jnp.concatenate(o_mem, axis=-1)
        gate_c = proj_scr[rows, _P_GATE_C:_P_GATE_C + XATTN_WIDTH]
        y_scr[rows, LRU_WIDTH + SWA_WIDTH:] = _gated(y_c * _row_rms_scale(y_c), gate_c)

    o_ref[0] = x + _dot(y_scr[...], w_out_ref[...])


def _rope_tables(seq):
    pos = np.arange(seq, dtype=np.float32)
    inv_freq = np.float32(ROPE_THETA) ** (-(np.arange(0, ROPE_DIM, 2, dtype=np.float32) / np.float32(ROPE_DIM)))
    ang = (pos[:, None] * inv_freq[None, :].astype(np.float32)).astype(np.float32)
    cos, sin = np.cos(ang).astype(np.float32), np.sin(ang).astype(np.float32)
    ones = np.ones((seq, HEAD_DIM - ROPE_DIM), np.float32)
    cos_head = np.concatenate([cos, cos, ones], axis=-1)
    sin_head = np.concatenate([-sin, sin, 0.0 * ones], axis=-1)
    reps = LANES // HEAD_DIM
    return jnp.asarray(np.tile(cos_head, (1, reps))), jnp.asarray(np.tile(sin_head, (1, reps)))


def _head_mean_matrix(width):
    block = np.full((HEAD_DIM, HEAD_DIM), 1.0 / HEAD_DIM, np.float32)
    return jnp.asarray(np.kron(np.eye(width // HEAD_DIM, dtype=np.float32), block), dtype=_BF16)


def _const_spec(shape):
    return pl.BlockSpec(shape, lambda *_: (0,) * len(shape))


def _gate_half_columns():
    scale = np.ones((LRU_WIDTH + _P_END,), np.float32)
    for start, width in ((_P_GATE_A, LRU_WIDTH), (_P_GATE_B, SWA_WIDTH), (_P_GATE_C, XATTN_WIDTH)):
        scale[LRU_WIDTH + start:LRU_WIDTH + start + width] = 0.5
    return scale


def _scale_gate_columns(w, first_col):
    half = _gate_half_columns()
    pieces = []
    for c in range(0, w.shape[1], LANES):
        scale = half[first_col + c:first_col + c + LANES]
        assert (scale == scale[0]).all()
        piece = w[:, c:c + LANES]
        pieces.append(piece if scale[0] == 1.0 else piece * float(scale[0]))
    return jnp.concatenate(pieces, axis=-1)


def _layer(x, mem, p, layer, *, tile):
    B, S, _ = x.shape
    n_tiles = S // tile
    assert tile >= D_MODEL and tile >= D_MIX and 2 * XATTN_WIDTH == LRU_WIDTH
    cos, sin = _rope_tables(S)
    vector_names = ("mem_norm_g", "xk_norm_g", "norm_g", "conv_w", "conv_b", "b_rg", "b_ig",
                    "lru_lambda", "q_norm_g", "k_norm_g", "xq_norm_g", "out_norm_g", "w_rg", "w_ig")
    vectors = [p[name].astype(_F32) for name in vector_names]
    big_weights = [p[name].astype(_F32) for name in ("w_mem_kv", "w_in", "w_out")]
    head_mean = _head_mean_matrix(LANES)
    operands = [p["sinks"].astype(_F32), x, mem, cos, sin] + vectors + [head_mean] + big_weights
    layer_block = lambda a: pl.BlockSpec((1,) + a.shape[1:], lambda b, s: (layer,) + (0,) * (a.ndim - 1))
    in_specs = [
        pl.BlockSpec(memory_space=pltpu.SMEM),
        pl.BlockSpec((1, tile, D_MODEL), lambda b, s: (b, s, 0)),
        pl.BlockSpec((1, MEM_LEN, D_MODEL), lambda b, s: (b, 0, 0)),
        pl.BlockSpec((tile, LANES), lambda b, s: (s, 0)), pl.BlockSpec((tile, LANES), lambda b, s: (s, 0)),
    ] + [layer_block(v) for v in vectors] + [_const_spec(head_mean.shape)] + [
        pl.BlockSpec(memory_space=pl.ANY) for _ in big_weights]

    return pl.pallas_call(
        functools.partial(_layer_kernel, tile=tile, layer=layer),
        out_shape=jax.ShapeDtypeStruct(x.shape, x.dtype),
        grid=(B, n_tiles),
        in_specs=in_specs,
        out_specs=pl.BlockSpec((1, tile, D_MODEL), lambda b, s: (b, s, 0)),
        scratch_shapes=[
            pltpu.VMEM((tile, _P_END), _F32),
            pltpu.VMEM((LRU_SLABS, SUBLANES * SEG_PITCH, LANES), _F32),
            pltpu.VMEM((LRU_SLABS, SUBLANES * SEG_PITCH, LANES), _F32),
            pltpu.VMEM((1, LRU_WIDTH), _F32),
            pltpu.VMEM(((CONV_WIDTH - 1) * SUBLANES, LRU_WIDTH), _F32),
            pltpu.VMEM((tile + BLOCK, LANES), _BF16),
            pltpu.VMEM((tile + BLOCK, LANES), _BF16),
            pltpu.VMEM((tile, D_MIX), _BF16),
            pltpu.VMEM((MEM_LEN, XATTN_WIDTH), _BF16),
            pltpu.VMEM((MEM_LEN, XATTN_WIDTH), _BF16),
            pltpu.VMEM((D_MODEL, 2 * XATTN_WIDTH), _BF16),
            pltpu.VMEM((D_MODEL, LRU_WIDTH + _P_END), _BF16),
            pltpu.VMEM((LRU_WIDTH // MXU_DIM, MXU_DIM, 2 * MXU_DIM), _BF16),
            pltpu.VMEM((D_MIX, D_MODEL), _BF16),
            pltpu.SemaphoreType.DMA((N_WEIGHT_COPIES,)),
        ],
        compiler_params=pltpu.CompilerParams(
            dimension_semantics=("arbitrary", "arbitrary"), vmem_limit_bytes=VMEM_LIMIT_BYTES),
        name="hymba_layer",
    )(*operands)


def kernel(x, mem, norm_g, mem_norm_g, w_in, conv_w, conv_b, w_rg, b_rg, w_ig, b_ig, lru_lambda, q_norm_g,
           k_norm_g, sinks, w_mem_kv, xq_norm_g, xk_norm_g, out_norm_g, w_out):
    depth = w_in.shape[0]
    tile = TILE
    assert x.shape[1] % tile == 0 and tile % BLOCK == 0 and WINDOW == BLOCK
    assert SWA_KV_HEADS == LANES // HEAD_DIM == 2
    assert all(hi == h // (SWA_Q_HEADS // SWA_KV_HEADS) for h, (_, hi) in enumerate(_HEAD_SLOTS))
    params = dict(
        norm_g=norm_g, w_in=w_in, conv_w=conv_w, conv_b=conv_b, w_rg=w_rg, b_rg=b_rg, w_ig=w_ig, b_ig=b_ig,
        lru_lambda=lru_lambda, q_norm_g=q_norm_g, k_norm_g=k_norm_g, sinks=sinks, xq_norm_g=xq_norm_g,
        out_norm_g=out_norm_g, w_out=w_out, mem_norm_g=mem_norm_g, w_mem_kv=w_mem_kv, xk_norm_g=xk_norm_g)
    h = x
    for l in range(depth):
        h = _layer(h, mem, params, l, tile=tile)
    return h
```

```python
import functools
import math

import jax
import jax.numpy as jnp
import numpy as np
from jax import lax
from jax.experimental import pallas as pl
from jax.experimental.pallas import tpu as pltpu

D_MODEL = 1024
MEM_LEN = 256
HEAD_DIM = 64
LRU_WIDTH = 512
LRU_BLOCKS = 8
LRU_BLOCK = LRU_WIDTH // LRU_BLOCKS
CONV_WIDTH = 4
LRU_C = 8.0
SWA_Q_HEADS = 4
SWA_KV_HEADS = 2
SWA_WIDTH = SWA_Q_HEADS * HEAD_DIM
SWA_KV_WIDTH = SWA_KV_HEADS * HEAD_DIM
WINDOW = 128
BLOCK = 128
XATTN_HEADS = 4
XATTN_WIDTH = XATTN_HEADS * HEAD_DIM
D_MIX = LRU_WIDTH + SWA_WIDTH + XATTN_WIDTH
ROPE_THETA = 500000.0
ROPE_DIM = HEAD_DIM // 4
EPS = 1e-6
NEG_INF = -1e30
LOG2_E = math.log2(math.e)

LANES = 128
SUBLANES = 8
MXU_DIM = 256
VMEM_LIMIT_BYTES = 56 * 1024 * 1024

TILE = 1024
SEG_LEN = TILE // SUBLANES
SEG_PITCH = SEG_LEN + SUBLANES
LRU_SLABS = LRU_WIDTH // LANES
SQRT_CLAMP = float(np.finfo(np.float32).tiny)
W_IN_CHUNK = 256
N_WEIGHT_COPIES = LRU_SLABS + (D_MODEL // W_IN_CHUNK) * (LRU_SLABS + 1) + 1

_P_GATE_A = 0
_P_Q = _P_GATE_A + LRU_WIDTH
_P_K = _P_Q + SWA_WIDTH
_P_V = _P_K + SWA_KV_WIDTH
_P_GATE_B = _P_V + SWA_KV_WIDTH
_P_XQ = _P_GATE_B + SWA_WIDTH
_P_GATE_C = _P_XQ + XATTN_WIDTH
_P_END = _P_GATE_C + XATTN_WIDTH

_HEAD_SLOTS = ((0, 0), (0, 1), (1, 0), (1, 1))
_HEAD_PAIRS = ((0, 3), (1, 2))

_BF16 = jnp.bfloat16
_F32 = jnp.float32


def _dot(a, b):
    return jnp.dot(a, b, preferred_element_type=_F32)


def _dot_nt(a, b):
    return lax.dot_general(a, b, (((1,), (1,)), ((), ())), preferred_element_type=_F32)


def _sigmoid_of_half(hz):
    return 0.5 * jnp.tanh(hz) + 0.5


def _gated(v, hz):
    hz = hz.astype(_BF16)
    return v.astype(_BF16) * (hz + hz * jnp.tanh(hz))


def _sqrt_nonneg(v):
    return v * lax.rsqrt(jnp.maximum(v, SQRT_CLAMP))


def _row_rms_scale(v):
    return lax.rsqrt(jnp.mean(v * v, axis=-1, keepdims=True) + EPS)


def _head_rms_scale(v, head_mean_ref):
    sq = (v * v).astype(_BF16)
    return lax.rsqrt(_dot(sq, head_mean_ref[...]) + EPS)


def _rope(v, cos, sin, partner_ref):
    partner = _dot(v.astype(_BF16), partner_ref[...])
    return v * cos + partner * sin


def _rope_partner_matrix():
    half = ROPE_DIM // 2
    p = np.zeros((LANES, LANES), np.float32)
    for l in range(LANES):
        d = l % HEAD_DIM
        if d < half:
            p[l + half, l] = 1.0
        elif d < ROPE_DIM:
            p[l - half, l] = 1.0
    return jnp.asarray(p, dtype=_BF16)


def _layer_kernel(
    sinks_ref, x_ref, mem_ref, cos_ref, sin_ref, mem_g_ref, xkg_ref, norm_g_ref, conv_w3_ref, conv_b_ref,
    b_rg_ref, b_ig_ref, lam_ref, qg_ref, kg_ref, xqg_ref, out_g_ref, w_rg_ref, w_ig_ref, head_mean_ref, partner_ref,
    w_mem_hbm, w_in_hbm, w_out_hbm,
    o_ref,
    proj_scr, conv_scr, h_scr, carry_scr, tail_scr, k_scr, v_scr, y_scr, km_scr, vm_scr,
    w_mem_ref, w_in_ref, w_gate_ref, w_out_ref, copy_sems,
    *, tile, layer,
):
    s = pl.program_id(1)
    n_blocks = tile // BLOCK

    first_step = (pl.program_id(0) == 0) & (s == 0)

    def weight_copies():
        n_sems = [0]

        def copy(src, dst):
            n_sems[0] += 1
            return pltpu.make_async_copy(src, dst, copy_sems.at[n_sems[0] - 1])

        rows = pl.ds(0, D_MODEL)
        mem_copies = [copy(w_mem_hbm.at[layer, :, pl.ds(slab * LANES, LANES)], h_scr.at[slab, rows, :])
                      for slab in range(LRU_SLABS)]
        in_chunks = []
        for r0 in range(0, D_MODEL, W_IN_CHUNK):
            rows = pl.ds(r0, W_IN_CHUNK)
            in_chunks.append(
                [copy(w_in_hbm.at[layer, rows, pl.ds(slab * LANES, LANES)], conv_scr.at[slab, rows, :])
                 for slab in range(LRU_SLABS)]
                + [copy(w_in_hbm.at[layer, rows, pl.ds(LRU_WIDTH, _P_END)], proj_scr.at[rows, :])])
        out_copy = copy(w_out_hbm.at[layer], o_ref.at[0, pl.ds(0, D_MIX), :])
        assert n_sems[0] == N_WEIGHT_COPIES
        return mem_copies, in_chunks, out_copy

    def gain_columns(g_ref, r0, width):
        col = jnp.broadcast_to(g_ref[:, r0:r0 + LANES], (LANES, LANES)).T
        return jnp.concatenate([col] * (width // LANES), axis=-1)

    @pl.when(first_step)
    def _prepare_weights():
        mem_copies, in_chunks, out_copy = weight_copies()
        for copy in mem_copies + sum(in_chunks, []) + [out_copy]:
            copy.start()

        spread = (lax.broadcasted_iota(jnp.int32, (LRU_BLOCK, MXU_DIM), 1) % LRU_BLOCK
                  == lax.broadcasted_iota(jnp.int32, (LRU_BLOCK, MXU_DIM), 0)).astype(_BF16)
        on_diagonal = (lax.broadcasted_iota(jnp.int32, (MXU_DIM, MXU_DIM), 0) // LRU_BLOCK
                       == lax.broadcasted_iota(jnp.int32, (MXU_DIM, MXU_DIM), 1) // LRU_BLOCK)
        for part, w_ref in enumerate((w_rg_ref, w_ig_ref)):
            blocks = (0.5 * w_ref[0].reshape(LRU_WIDTH, LRU_BLOCK)).astype(_BF16)
            for j in range(LRU_WIDTH // MXU_DIM):
                wide = _dot(blocks[j * MXU_DIM:(j + 1) * MXU_DIM, :], spread)
                w_gate_ref[j, :, part * MXU_DIM:(part + 1) * MXU_DIM] = jnp.where(on_diagonal, wide, 0.0).astype(_BF16)

        for copy in mem_copies:
            copy.wait()
        for r0 in range(0, D_MODEL, LANES):
            w_rows = jnp.concatenate([h_scr[slab, r0:r0 + LANES, :] for slab in range(LRU_SLABS)], axis=-1)
            w_mem_ref[r0:r0 + LANES, :] = (gain_columns(mem_g_ref, r0, 2 * XATTN_WIDTH) * w_rows).astype(_BF16)
        for r0 in range(0, D_MODEL, LANES):
            if r0 % W_IN_CHUNK == 0:
                for copy in in_chunks[r0 // W_IN_CHUNK]:
                    copy.wait()
            lru_cols = jnp.concatenate([conv_scr[slab, r0:r0 + LANES, :] for slab in range(LRU_SLABS)], axis=-1)
            w_rows = jnp.concatenate([_scale_gate_columns(lru_cols, 0),
                                      _scale_gate_columns(proj_scr[r0:r0 + LANES, :], LRU_WIDTH)], axis=-1)
            w_in_ref[r0:r0 + LANES, :] = (gain_columns(norm_g_ref, r0, LRU_WIDTH + _P_END) * w_rows).astype(_BF16)
        out_copy.wait()
        for r0 in range(0, D_MIX, LANES):
            w_out_ref[r0:r0 + LANES, :] = (gain_columns(out_g_ref, r0, D_MODEL) * o_ref[0, r0:r0 + LANES, :]).astype(_BF16)

    conv_w_ref = conv_w3_ref.at[0]
    score_scale = LOG2_E / math.sqrt(HEAD_DIM)
    two_heads = lambda g_ref: jnp.concatenate([g_ref[...]] * (LANES // HEAD_DIM), axis=1)
    q_gain = two_heads(qg_ref) * score_scale
    k_gain = two_heads(kg_ref)
    xq_gain = two_heads(xqg_ref) * score_scale

    @pl.when(s == 0)
    def _reset_sequence_state():
        tail_scr[...] = jnp.zeros_like(tail_scr)
        carry_scr[...] = jnp.zeros_like(carry_scr)
        k_scr[:, tile:tile + BLOCK, :] = jnp.zeros((SWA_KV_HEADS, BLOCK, LANES), _BF16)
        v_scr[:, tile:tile + BLOCK, :] = jnp.zeros((SWA_KV_HEADS, BLOCK, LANES), _BF16)

    @pl.when(s == 0)
    def _project_memory():
        m = mem_ref[0]
        mn = (m * _row_rms_scale(m)).astype(_BF16)
        kv = _dot(mn, w_mem_ref[...])
        xk_gain = two_heads(xkg_ref) * xq_gain
        for g in range(XATTN_WIDTH // LANES):
            k = kv[:, g * LANES:(g + 1) * LANES]
            km_scr[:, g * LANES:(g + 1) * LANES] = (k * _head_rms_scale(k, head_mean_ref) * xk_gain).astype(_BF16)
        vm_scr[...] = kv[:, XATTN_WIDTH:].astype(_BF16)

    k_scr[:, 0:BLOCK, :] = k_scr[:, tile:tile + BLOCK, :]
    v_scr[:, 0:BLOCK, :] = v_scr[:, tile:tile + BLOCK, :]

    x = x_ref[0]
    xn = (x * _row_rms_scale(x)).astype(_BF16)
    lru_x = _dot(xn, w_in_ref[:, 0:LRU_WIDTH])
    for seg in range(SUBLANES):
        for slab in range(LRU_SLABS):
            conv_scr[slab, seg * SEG_PITCH:seg * SEG_PITCH + SEG_LEN, :] = (
                lru_x[seg * SEG_LEN:(seg + 1) * SEG_LEN, slab * LANES:(slab + 1) * LANES])
    proj_scr[...] = _dot(xn, w_in_ref[:, LRU_WIDTH:])

    seg_id = lax.broadcasted_iota(jnp.int32, (SUBLANES, LRU_WIDTH), 0)

    def strided(ref, g):
        return jnp.concatenate(
            [ref[slab, pl.ds(g, SUBLANES, stride=SEG_PITCH), :] for slab in range(LRU_SLABS)], axis=-1)

    def shift_segments(cur, prev_tile):
        return jnp.where(seg_id == 0, pltpu.roll(prev_tile, 1, axis=0), pltpu.roll(cur, 1, axis=0))

    u = [strided(conv_scr, g) for g in range(SEG_LEN)]
    wrapped = {}
    for k in range(1, CONV_WIDTH):
        wrapped[SEG_LEN - k] = shift_segments(u[SEG_LEN - k], tail_scr[(k - 1) * SUBLANES:k * SUBLANES, :])
    for k in range(1, CONV_WIDTH):
        tail_scr[(k - 1) * SUBLANES:k * SUBLANES, :] = u[SEG_LEN - k]
    xc_parts = []
    for g in range(SEG_LEN):
        xc = conv_b_ref[...] + conv_w_ref[CONV_WIDTH - 1:CONV_WIDTH, :] * u[g]
        for k in range(1, CONV_WIDTH):
            past = u[g - k] if g >= k else wrapped[SEG_LEN + g - k]
            xc = xc + conv_w_ref[CONV_WIDTH - 1 - k:CONV_WIDTH - k, :] * past
        xc_parts.append(xc)
    xc = jnp.concatenate(xc_parts, axis=0)

    lam = -lam_ref[...]
    softplus = jnp.maximum(lam, 0.0) + jnp.log1p(jnp.exp(-jnp.abs(lam)))
    half_c_softplus = 0.5 * LRU_C * softplus
    a_parts, b_parts = [], []
    for j in range(LRU_WIDTH // MXU_DIM):
        cols = slice(j * MXU_DIM, (j + 1) * MXU_DIM)
        pre = _dot(xc[:, cols].astype(_BF16), w_gate_ref[j])
        r_tanh = jnp.tanh(pre[:, :MXU_DIM] + 0.5 * b_rg_ref[:, cols])
        ig = _sigmoid_of_half(pre[:, MXU_DIM:] + 0.5 * b_ig_ref[:, cols])
        neg_log_a = r_tanh * half_c_softplus[:, cols] + half_c_softplus[:, cols]
        a = jnp.exp2(neg_log_a * -LOG2_E)
        a_parts.append(a)
        b_parts.append(_sqrt_nonneg(jnp.tanh(neg_log_a) * (1.0 + a * a)) * (ig * xc[:, cols]))
    a_all = jnp.concatenate(a_parts, axis=-1)
    b_all = jnp.concatenate(b_parts, axis=-1)

    local, decay = [], []
    for g in range(SEG_LEN):
        a_g = a_all[g * SUBLANES:(g + 1) * SUBLANES, :]
        b_g = b_all[g * SUBLANES:(g + 1) * SUBLANES, :]
        local.append(b_g if g == 0 else a_g * local[-1] + b_g)
        decay.append(a_g if g == 0 else a_g * decay[-1])
    seg_a, seg_b = decay[-1], local[-1]
    d = 1
    while d < SUBLANES:
        keep = seg_id >= d
        a_prev = jnp.where(keep, pltpu.roll(seg_a, d, axis=0), 1.0)
        b_prev = jnp.where(keep, pltpu.roll(seg_b, d, axis=0), 0.0)
        seg_b = seg_b + seg_a * b_prev
        seg_a = seg_a * a_prev
        d *= 2
    carry = carry_scr[...]
    seg_end = seg_a * carry + seg_b
    seg_in = jnp.where(seg_id == 0, carry, pltpu.roll(seg_end, 1, axis=0))
    carry_scr[...] = seg_end[SUBLANES - 1:SUBLANES, :]
    for g in range(SEG_LEN):
        h_g = local[g] + decay[g] * seg_in
        for slab in range(LRU_SLABS):
            h_scr[slab, pl.ds(g, SUBLANES, stride=SEG_PITCH), :] = h_g[:, slab * LANES:(slab + 1) * LANES]

    h = jnp.concatenate([
        jnp.concatenate([h_scr[slab, seg * SEG_PITCH:seg * SEG_PITCH + SEG_LEN, :] for slab in range(LRU_SLABS)],
                        axis=-1)
        for seg in range(SUBLANES)], axis=0)
    gate_a = proj_scr[:, _P_GATE_A:_P_GATE_A + LRU_WIDTH]
    y_scr[:, 0:LRU_WIDTH] = _gated(h * _row_rms_scale(h), gate_a)

    cos = cos_ref[...]
    sin = sin_ref[...]
    q_groups = []
    for g in range(SWA_WIDTH // LANES):
        cols = slice(_P_Q + g * LANES, _P_Q + (g + 1) * LANES)
        qraw = proj_scr[:, cols]
        qn = qraw * _head_rms_scale(qraw, head_mean_ref) * q_gain
        q_groups.append(_rope(qn, cos, sin, partner_ref))
    kraw = proj_scr[:, _P_K:_P_K + SWA_KV_WIDTH]
    kn = kraw * _head_rms_scale(kraw, head_mean_ref) * k_gain
    for name_scr, kv in ((k_scr, _rope(kn, cos, sin, partner_ref)), (v_scr, proj_scr[:, _P_V:_P_V + SWA_KV_WIDTH])):
        name_scr[0, BLOCK:BLOCK + tile, :] = kv.astype(_BF16)
        name_scr[1, BLOCK:BLOCK + tile, :] = pltpu.roll(kv, HEAD_DIM, axis=1).astype(_BF16)

    xq_groups = []
    for g in range(XATTN_WIDTH // LANES):
        cols = slice(_P_XQ + g * LANES, _P_XQ + (g + 1) * LANES)
        xraw = proj_scr[:, cols]
        xq_groups.append(xraw * _head_rms_scale(xraw, head_mean_ref))

    lane = lax.broadcasted_iota(jnp.int32, (BLOCK, LANES), 1)
    low_half = lane < HEAD_DIM
    half_masks = (low_half, lane >= HEAD_DIM)
    qi = lax.broadcasted_iota(jnp.int32, (BLOCK, BLOCK), 0)
    kj = lax.broadcasted_iota(jnp.int32, (BLOCK, BLOCK), 1)
    own_block = kj <= qi

    for blk in range(n_blocks):
        rows = slice(blk * BLOCK, (blk + 1) * BLOCK)
        o_head = [None] * SWA_Q_HEADS
        for kv_copy, heads in enumerate(_HEAD_PAIRS):
            kk = k_scr[kv_copy, blk * BLOCK:(blk + 2) * BLOCK, :]
            vv = v_scr[kv_copy, blk * BLOCK:(blk + 2) * BLOCK, :]
            qm = jnp.concatenate([
                jnp.where(half_masks[_HEAD_SLOTS[head][1]], q_groups[_HEAD_SLOTS[head][0]][rows, :], 0.0).astype(_BF16)
                for head in heads], axis=0)
            sc_pair = _dot_nt(qm, kk)
            p_pair, dens = [], []
            for i, head in enumerate(heads):
                sc_both = sc_pair[i * BLOCK:(i + 1) * BLOCK, :]
                sc_prev = sc_both[:, :BLOCK]
                if blk == 0:
                    sc_prev = jnp.where(s > 0, sc_prev, NEG_INF)
                sc = jnp.where(own_block, sc_both[:, BLOCK:], sc_prev)
                sink = sinks_ref[layer, head] * LOG2_E
                m = jnp.maximum(jnp.max(sc, axis=-1, keepdims=True), sink)
                p = jnp.exp2(sc - m)
                dens.append(jnp.sum(p, axis=-1, keepdims=True) + jnp.exp2(sink - m))
                p_pair.append(jnp.concatenate(
                    [jnp.where(own_block, 0.0, p), jnp.where(own_block, p, 0.0)], axis=-1).astype(_BF16))
            o_pair = _dot(jnp.concatenate(p_pair, axis=0), vv)
            for i, head in enumerate(heads):
                o_head[head] = o_pair[i * BLOCK:(i + 1) * BLOCK, :] * (1.0 / dens[i])
        o_swa = [jnp.where(low_half, o_head[2 * g], o_head[2 * g + 1]) for g in range(SWA_WIDTH // LANES)]
        y_b = jnp.concatenate(o_swa, axis=-1)
        gate_b = proj_scr[rows, _P_GATE_B:_P_GATE_B + SWA_WIDTH]
        y_scr[rows, LRU_WIDTH:LRU_WIDTH + SWA_WIDTH] = _gated(y_b * _row_rms_scale(y_b), gate_b)

        o_mem = []
        for g in range(XATTN_WIDTH // LANES):
            qm = jnp.concatenate([jnp.where(half_masks[hi], xq_groups[g][rows, :], 0.0).astype(_BF16)
                                  for hi in range(LANES // HEAD_DIM)], axis=0)
            sc = _dot_nt(qm, km_scr[:, g * LANES:(g + 1) * LANES])
            m = jnp.max(sc, axis=-1, keepdims=True)
            p = jnp.exp2(sc - m)
            den = jnp.sum(p, axis=-1, keepdims=True)
            o = _dot(p.astype(_BF16), vm_scr[:, g * LANES:(g + 1) * LANES]) * (1.0 / den)
            o_mem.append(jnp.where(low_half, o[:BLOCK, :], o[BLOCK:, :]))
        y_c = jnp.concatenate(o_mem, axis=-1)
        gate_c = proj_scr[rows, _P_GATE_C:_P_GATE_C + XATTN_WIDTH]
        y_scr[rows, LRU_WIDTH + SWA_WIDTH:] = _gated(y_c * _row_rms_scale(y_c), gate_c)

    o_ref[0] = x + _dot(y_scr[...], w_out_ref[...])


def _rope_tables(seq):
    pos = np.arange(seq, dtype=np.float32)
    inv_freq = np.float32(ROPE_THETA) ** (-(np.arange(0, ROPE_DIM, 2, dtype=np.float32) / np.float32(ROPE_DIM)))
    ang = (pos[:, None] * inv_freq[None, :].astype(np.float32)).astype(np.float32)
    cos, sin = np.cos(ang).astype(np.float32), np.sin(ang).astype(np.float32)
    ones = np.ones((seq, HEAD_DIM - ROPE_DIM), np.float32)
    cos_head = np.concatenate([cos, cos, ones], axis=-1)
    sin_head = np.concatenate([-sin, sin, 0.0 * ones], axis=-1)
    reps = LANES // HEAD_DIM
    return jnp.asarray(np.tile(cos_head, (1, reps))), jnp.asarray(np.tile(sin_head, (1, reps)))


def _head_mean_matrix(width):
    block = np.full((HEAD_DIM, HEAD_DIM), 1.0 / HEAD_DIM, np.float32)
    return jnp.asarray(np.kron(np.eye(width // HEAD_DIM, dtype=np.float32), block), dtype=_BF16)


def _const_spec(shape):
    return pl.BlockSpec(shape, lambda *_: (0,) * len(shape))


def _gate_half_columns():
    scale = np.ones((LRU_WIDTH + _P_END,), np.float32)
    for start, width in ((_P_GATE_A, LRU_WIDTH), (_P_GATE_B, SWA_WIDTH), (_P_GATE_C, XATTN_WIDTH)):
        scale[LRU_WIDTH + start:LRU_WIDTH + start + width] = 0.5
    return scale


def _scale_gate_columns(w, first_col):
    half = _gate_half_columns()
    pieces = []
    for c in range(0, w.shape[1], LANES):
        scale = half[first_col + c:first_col + c + LANES]
        assert (scale == scale[0]).all()
        piece = w[:, c:c + LANES]
        pieces.append(piece if scale[0] == 1.0 else piece * float(scale[0]))
    return jnp.concatenate(pieces, axis=-1)


def _layer(x, mem, p, layer, *, tile):
    B, S, _ = x.shape
    n_tiles = S // tile
    assert tile >= D_MODEL and tile >= D_MIX and 2 * XATTN_WIDTH == LRU_WIDTH
    cos, sin = _rope_tables(S)
    vector_names = ("mem_norm_g", "xk_norm_g", "norm_g", "conv_w", "conv_b", "b_rg", "b_ig",
                    "lru_lambda", "q_norm_g", "k_norm_g", "xq_norm_g", "out_norm_g", "w_rg", "w_ig")
    vectors = [p[name].astype(_F32) for name in vector_names]
    big_weights = [p[name].astype(_F32) for name in ("w_mem_kv", "w_in", "w_out")]
    head_mean = _head_mean_matrix(LANES)
    partner = _rope_partner_matrix()
    operands = [p["sinks"].astype(_F32), x, mem, cos, sin] + vectors + [head_mean, partner] + big_weights
    layer_block = lambda a: pl.BlockSpec((1,) + a.shape[1:], lambda b, s: (layer,) + (0,) * (a.ndim - 1))
    in_specs = [
        pl.BlockSpec(memory_space=pltpu.SMEM),
        pl.BlockSpec((1, tile, D_MODEL), lambda b, s: (b, s, 0)),
        pl.BlockSpec((1, MEM_LEN, D_MODEL), lambda b, s: (b, 0, 0)),
        pl.BlockSpec((tile, LANES), lambda b, s: (s, 0)), pl.BlockSpec((tile, LANES), lambda b, s: (s, 0)),
    ] + [layer_block(v) for v in vectors] + [_const_spec(head_mean.shape), _const_spec(partner.shape)] + [
        pl.BlockSpec(memory_space=pl.ANY) for _ in big_weights]

    return pl.pallas_call(
        functools.partial(_layer_kernel, tile=tile, layer=layer),
        out_shape=jax.ShapeDtypeStruct(x.shape, x.dtype),
        grid=(B, n_tiles),
        in_specs=in_specs,
        out_specs=pl.BlockSpec((1, tile, D_MODEL), lambda b, s: (b, s, 0)),
        scratch_shapes=[
            pltpu.VMEM((tile, _P_END), _F32),
            pltpu.VMEM((LRU_SLABS, SUBLANES * SEG_PITCH, LANES), _F32),
            pltpu.VMEM((LRU_SLABS, SUBLANES * SEG_PITCH, LANES), _F32),
            pltpu.VMEM((1, LRU_WIDTH), _F32),
            pltpu.VMEM(((CONV_WIDTH - 1) * SUBLANES, LRU_WIDTH), _F32),
            pltpu.VMEM((SWA_KV_HEADS, tile + BLOCK, LANES), _BF16),
            pltpu.VMEM((SWA_KV_HEADS, tile + BLOCK, LANES), _BF16),
            pltpu.VMEM((tile, D_MIX), _BF16),
            pltpu.VMEM((MEM_LEN, XATTN_WIDTH), _BF16),
            pltpu.VMEM((MEM_LEN, XATTN_WIDTH), _BF16),
            pltpu.VMEM((D_MODEL, 2 * XATTN_WIDTH), _BF16),
            pltpu.VMEM((D_MODEL, LRU_WIDTH + _P_END), _BF16),
            pltpu.VMEM((LRU_WIDTH // MXU_DIM, MXU_DIM, 2 * MXU_DIM), _BF16),
            pltpu.VMEM((D_MIX, D_MODEL), _BF16),
            pltpu.SemaphoreType.DMA((N_WEIGHT_COPIES,)),
        ],
        compiler_params=pltpu.CompilerParams(
            dimension_semantics=("arbitrary", "arbitrary"), vmem_limit_bytes=VMEM_LIMIT_BYTES),
        name="hymba_layer",
    )(*operands)


def kernel(x, mem, norm_g, mem_norm_g, w_in, conv_w, conv_b, w_rg, b_rg, w_ig, b_ig, lru_lambda, q_norm_g,
           k_norm_g, sinks, w_mem_kv, xq_norm_g, xk_norm_g, out_norm_g, w_out):
    depth = w_in.shape[0]
    tile = TILE
    assert x.shape[1] % tile == 0 and tile % BLOCK == 0 and WINDOW == BLOCK
    assert SWA_KV_HEADS == LANES // HEAD_DIM == 2
    params = dict(
        norm_g=norm_g, w_in=w_in, conv_w=conv_w, conv_b=conv_b, w_rg=w_rg, b_rg=b_rg, w_ig=w_ig, b_ig=b_ig,
        lru_lambda=lru_lambda, q_norm_g=q_norm_g, k_norm_g=k_norm_g, sinks=sinks, xq_norm_g=xq_norm_g,
        out_norm_g=out_norm_g, w_out=w_out, mem_norm_g=mem_norm_g, w_mem_kv=w_mem_kv, xk_norm_g=xk_norm_g)
    h = x
    for l in range(depth):
        h = _layer(h, mem, params, l, tile=tile)
    return h
```

```python
import functools
import math

import jax
import jax.numpy as jnp
import numpy as np
from jax import lax
from jax.experimental import pallas as pl
from jax.experimental.pallas import tpu as pltpu

D_MODEL = 1024
MEM_LEN = 256
HEAD_DIM = 64
LRU_WIDTH = 512
LRU_BLOCKS = 8
LRU_BLOCK = LRU_WIDTH // LRU_BLOCKS
CONV_WIDTH = 4
LRU_C = 8.0
SWA_Q_HEADS = 4
SWA_KV_HEADS = 2
SWA_WIDTH = SWA_Q_HEADS * HEAD_DIM
SWA_KV_WIDTH = SWA_KV_HEADS * HEAD_DIM
WINDOW = 128
BLOCK = 128
XATTN_HEADS = 4
XATTN_WIDTH = XATTN_HEADS * HEAD_DIM
D_MIX = LRU_WIDTH + SWA_WIDTH + XATTN_WIDTH
ROPE_THETA = 500000.0
ROPE_DIM = HEAD_DIM // 4
EPS = 1e-6
NEG_INF = -1e30
LOG2_E = math.log2(math.e)

LANES = 128
SUBLANES = 8
MXU_DIM = 256
VMEM_LIMIT_BYTES = 56 * 1024 * 1024

TILE = 1024
SEG_LEN = TILE // SUBLANES
SEG_PITCH = SEG_LEN + SUBLANES
LRU_SLABS = LRU_WIDTH // LANES
SQRT_CLAMP = float(np.finfo(np.float32).tiny)
W_IN_CHUNK = 256
N_WEIGHT_COPIES = LRU_SLABS + (D_MODEL // W_IN_CHUNK) * (LRU_SLABS + 1) + 1

_P_GATE_A = 0
_P_Q = _P_GATE_A + LRU_WIDTH
_P_K = _P_Q + SWA_WIDTH
_P_V = _P_K + SWA_KV_WIDTH
_P_GATE_B = _P_V + SWA_KV_WIDTH
_P_XQ = _P_GATE_B + SWA_WIDTH
_P_GATE_C = _P_XQ + XATTN_WIDTH
_P_END = _P_GATE_C + XATTN_WIDTH

_HEAD_SLOTS = ((0, 0), (0, 1), (1, 0), (1, 1))
_HEAD_PAIRS = ((0, 3), (1, 2))

_BF16 = jnp.bfloat16
_F32 = jnp.float32


def _dot(a, b):
    return jnp.dot(a, b, preferred_element_type=_F32)


def _dot_nt(a, b):
    return lax.dot_general(a, b, (((1,), (1,)), ((), ())), preferred_element_type=_F32)


def _sigmoid_of_half(hz):
    return 0.5 * jnp.tanh(hz) + 0.5


def _gated(v, hz):
    hz = hz.astype(_BF16)
    return v.astype(_BF16) * (hz + hz * jnp.tanh(hz))


def _sqrt_nonneg(v):
    return v * lax.rsqrt(jnp.maximum(v, SQRT_CLAMP))


def _row_rms_scale(v):
    return lax.rsqrt(jnp.mean(v * v, axis=-1, keepdims=True) + EPS)


def _head_rms_scale(v, head_mean_ref):
    sq = (v * v).astype(_BF16)
    return lax.rsqrt(_dot(sq, head_mean_ref[...]) + EPS)


def _rope(v, cos, sin, partner_ref):
    partner = _dot(v.astype(_BF16), partner_ref[...])
    return v * cos + partner * sin


def _rope_partner_matrix():
    half = ROPE_DIM // 2
    p = np.zeros((LANES, LANES), np.float32)
    for l in range(LANES):
        d = l % HEAD_DIM
        if d < half:
            p[l + half, l] = 1.0
        elif d < ROPE_DIM:
            p[l - half, l] = 1.0
    return jnp.asarray(p, dtype=_BF16)


def _layer_kernel(
    sinks_ref, x_ref, mem_ref, cos_ref, sin_ref, mem_g_ref, xkg_ref, norm_g_ref, conv_w3_ref, conv_b_ref,
    b_rg_ref, b_ig_ref, lam_ref, qg_ref, kg_ref, xqg_ref, out_g_ref, w_rg_ref, w_ig_ref, head_mean_ref, partner_ref,
    w_mem_hbm, w_in_hbm, w_out_hbm,
    o_ref,
    proj_scr, conv_scr, h_scr, carry_scr, tail_scr, k_scr, v_scr, y_scr, km_scr, vm_scr,
    w_mem_ref, w_in_ref, w_gate_ref, w_out_ref, copy_sems,
    *, tile, layer,
):
    s = pl.program_id(1)
    n_blocks = tile // BLOCK

    first_step = (pl.program_id(0) == 0) & (s == 0)

    def weight_copies():
        n_sems = [0]

        def copy(src, dst):
            n_sems[0] += 1
            return pltpu.make_async_copy(src, dst, copy_sems.at[n_sems[0] - 1])

        rows = pl.ds(0, D_MODEL)
        mem_copies = [copy(w_mem_hbm.at[layer, :, pl.ds(slab * LANES, LANES)], h_scr.at[slab, rows, :])
                      for slab in range(LRU_SLABS)]
        in_chunks = []
        for r0 in range(0, D_MODEL, W_IN_CHUNK):
            rows = pl.ds(r0, W_IN_CHUNK)
            in_chunks.append(
                [copy(w_in_hbm.at[layer, rows, pl.ds(slab * LANES, LANES)], conv_scr.at[slab, rows, :])
                 for slab in range(LRU_SLABS)]
                + [copy(w_in_hbm.at[layer, rows, pl.ds(LRU_WIDTH, _P_END)], proj_scr.at[rows, :])])
        out_copy = copy(w_out_hbm.at[layer], o_ref.at[0, pl.ds(0, D_MIX), :])
        assert n_sems[0] == N_WEIGHT_COPIES
        return mem_copies, in_chunks, out_copy

    def gain_columns(g_ref, r0, width):
        col = jnp.broadcast_to(g_ref[:, r0:r0 + LANES], (LANES, LANES)).T
        return jnp.concatenate([col] * (width // LANES), axis=-1)

    @pl.when(first_step)
    def _prepare_weights():
        mem_copies, in_chunks, out_copy = weight_copies()
        for copy in mem_copies + sum(in_chunks, []) + [out_copy]:
            copy.start()

        spread = (lax.broadcasted_iota(jnp.int32, (LRU_BLOCK, MXU_DIM), 1) % LRU_BLOCK
                  == lax.broadcasted_iota(jnp.int32, (LRU_BLOCK, MXU_DIM), 0)).astype(_BF16)
        on_diagonal = (lax.broadcasted_iota(jnp.int32, (MXU_DIM, MXU_DIM), 0) // LRU_BLOCK
                       == lax.broadcasted_iota(jnp.int32, (MXU_DIM, MXU_DIM), 1) // LRU_BLOCK)
        for part, w_ref in enumerate((w_rg_ref, w_ig_ref)):
            blocks = (0.5 * w_ref[0].reshape(LRU_WIDTH, LRU_BLOCK)).astype(_BF16)
            for j in range(LRU_WIDTH // MXU_DIM):
                wide = _dot(blocks[j * MXU_DIM:(j + 1) * MXU_DIM, :], spread)
                w_gate_ref[j, :, part * MXU_DIM:(part + 1) * MXU_DIM] = jnp.where(on_diagonal, wide, 0.0).astype(_BF16)

        for copy in mem_copies:
            copy.wait()
        for r0 in range(0, D_MODEL, LANES):
            w_rows = jnp.concatenate([h_scr[slab, r0:r0 + LANES, :] for slab in range(LRU_SLABS)], axis=-1)
            w_mem_ref[r0:r0 + LANES, :] = (gain_columns(mem_g_ref, r0, 2 * XATTN_WIDTH) * w_rows).astype(_BF16)
        for r0 in range(0, D_MODEL, LANES):
            if r0 % W_IN_CHUNK == 0:
                for copy in in_chunks[r0 // W_IN_CHUNK]:
                    copy.wait()
            lru_cols = jnp.concatenate([conv_scr[slab, r0:r0 + LANES, :] for slab in range(LRU_SLABS)], axis=-1)
            w_rows = jnp.concatenate([_scale_gate_columns(lru_cols, 0),
                                      _scale_gate_columns(proj_scr[r0:r0 + LANES, :], LRU_WIDTH)], axis=-1)
            w_in_ref[r0:r0 + LANES, :] = (gain_columns(norm_g_ref, r0, LRU_WIDTH + _P_END) * w_rows).astype(_BF16)
        out_copy.wait()
        for r0 in range(0, D_MIX, LANES):
            w_out_ref[r0:r0 + LANES, :] = (gain_columns(out_g_ref, r0, D_MODEL) * o_ref[0, r0:r0 + LANES, :]).astype(_BF16)

    conv_w_ref = conv_w3_ref.at[0]
    score_scale = LOG2_E / math.sqrt(HEAD_DIM)
    two_heads = lambda g_ref: jnp.concatenate([g_ref[...]] * (LANES // HEAD_DIM), axis=1)
    q_gain = two_heads(qg_ref) * score_scale
    k_gain = two_heads(kg_ref)
    xq_gain = two_heads(xqg_ref) * score_scale

    @pl.when(s == 0)
    def _reset_sequence_state():
        tail_scr[...] = jnp.zeros_like(tail_scr)
        carry_scr[...] = jnp.zeros_like(carry_scr)
        k_scr[:, tile:tile + BLOCK, :] = jnp.zeros((SWA_KV_HEADS, BLOCK, LANES), _BF16)
        v_scr[:, tile:tile + BLOCK, :] = jnp.zeros((SWA_KV_HEADS, BLOCK, LANES), _BF16)

    @pl.when(s == 0)
    def _project_memory():
        m = mem_ref[0]
        mn = (m * _row_rms_scale(m)).astype(_BF16)
        kv = _dot(mn, w_mem_ref[...])
        xk_gain = two_heads(xkg_ref) * xq_gain
        for g in range(XATTN_WIDTH // LANES):
            k = kv[:, g * LANES:(g + 1) * LANES]
            km_scr[:, g * LANES:(g + 1) * LANES] = (k * _head_rms_scale(k, head_mean_ref) * xk_gain).astype(_BF16)
        vm_scr[...] = kv[:, XATTN_WIDTH:].astype(_BF16)

    k_scr[:, 0:BLOCK, :] = k_scr[:, tile:tile + BLOCK, :]
    v_scr[:, 0:BLOCK, :] = v_scr[:, tile:tile + BLOCK, :]

    x = x_ref[0]
    xn = (x * _row_rms_scale(x)).astype(_BF16)
    lru_x = _dot(xn, w_in_ref[:, 0:LRU_WIDTH])
    for seg in range(SUBLANES):
        for slab in range(LRU_SLABS):
            conv_scr[slab, seg * SEG_PITCH:seg * SEG_PITCH + SEG_LEN, :] = (
                lru_x[seg * SEG_LEN:(seg + 1) * SEG_LEN, slab * LANES:(slab + 1) * LANES])
    proj_scr[...] = _dot(xn, w_in_ref[:, LRU_WIDTH:])

    seg_id = lax.broadcasted_iota(jnp.int32, (SUBLANES, LRU_WIDTH), 0)

    def strided(ref, g):
        return jnp.concatenate(
            [ref[slab, pl.ds(g, SUBLANES, stride=SEG_PITCH), :] for slab in range(LRU_SLABS)], axis=-1)

    def shift_segments(cur, prev_tile):
        return jnp.where(seg_id == 0, pltpu.roll(prev_tile, 1, axis=0), pltpu.roll(cur, 1, axis=0))

    u = [strided(conv_scr, g) for g in range(SEG_LEN)]
    wrapped = {}
    for k in range(1, CONV_WIDTH):
        wrapped[SEG_LEN - k] = shift_segments(u[SEG_LEN - k], tail_scr[(k - 1) * SUBLANES:k * SUBLANES, :])
    for k in range(1, CONV_WIDTH):
        tail_scr[(k - 1) * SUBLANES:k * SUBLANES, :] = u[SEG_LEN - k]
    xc_parts = []
    for g in range(SEG_LEN):
        xc = conv_b_ref[...] + conv_w_ref[CONV_WIDTH - 1:CONV_WIDTH, :] * u[g]
        for k in range(1, CONV_WIDTH):
            past = u[g - k] if g >= k else wrapped[SEG_LEN + g - k]
            xc = xc + conv_w_ref[CONV_WIDTH - 1 - k:CONV_WIDTH - k, :] * past
        xc_parts.append(xc)
    xc = jnp.concatenate(xc_parts, axis=0)

    lam = -lam_ref[...]
    softplus = jnp.maximum(lam, 0.0) + jnp.log1p(jnp.exp(-jnp.abs(lam)))
    half_c_softplus = 0.5 * LRU_C * softplus
    a_parts, b_parts = [], []
    for j in range(LRU_WIDTH // MXU_DIM):
        cols = slice(j * MXU_DIM, (j + 1) * MXU_DIM)
        pre = _dot(xc[:, cols].astype(_BF16), w_gate_ref[j])
        r_tanh = jnp.tanh(pre[:, :MXU_DIM] + 0.5 * b_rg_ref[:, cols])
        ig = _sigmoid_of_half(pre[:, MXU_DIM:] + 0.5 * b_ig_ref[:, cols])
        neg_log_a = r_tanh * half_c_softplus[:, cols] + half_c_softplus[:, cols]
        a = jnp.exp2(neg_log_a * -LOG2_E)
        a_parts.append(a)
        b_parts.append(_sqrt_nonneg(jnp.tanh(neg_log_a) * (1.0 + a * a)) * (ig * xc[:, cols]))
    a_all = jnp.concatenate(a_parts, axis=-1)
    b_all = jnp.concatenate(b_parts, axis=-1)

    local, decay = [], []
    for g in range(SEG_LEN):
        a_g = a_all[g * SUBLANES:(g + 1) * SUBLANES, :]
        b_g = b_all[g * SUBLANES:(g + 1) * SUBLANES, :]
        local.append(b_g if g == 0 else a_g * local[-1] + b_g)
        decay.append(a_g if g == 0 else a_g * decay[-1])
    seg_a, seg_b = decay[-1], local[-1]
    d = 1
    while d < SUBLANES:
        keep = seg_id >= d
        a_prev = jnp.where(keep, pltpu.roll(seg_a, d, axis=0), 1.0)
        b_prev = jnp.where(keep, pltpu.roll(seg_b, d, axis=0), 0.0)
        seg_b = seg_b + seg_a * b_prev
        seg_a = seg_a * a_prev
        d *= 2
    carry = carry_scr[...]
    seg_end = seg_a * carry + seg_b
    seg_in = jnp.where(seg_id == 0, carry, pltpu.roll(seg_end, 1, axis=0))
    carry_scr[...] = seg_end[SUBLANES - 1:SUBLANES, :]
    for g in range(SEG_LEN):
        h_g = local[g] + decay[g] * seg_in
        for slab in range(LRU_SLABS):
            h_scr[slab, pl.ds(g, SUBLANES, stride=SEG_PITCH), :] = h_g[:, slab * LANES:(slab + 1) * LANES]

    h = jnp.concatenate([
        jnp.concatenate([h_scr[slab, seg * SEG_PITCH:seg * SEG_PITCH + SEG_LEN, :] for slab in range(LRU_SLABS)],
                        axis=-1)
        for seg in range(SUBLANES)], axis=0)
    gate_a = proj_scr[:, _P_GATE_A:_P_GATE_A + LRU_WIDTH]
    y_scr[:, 0:LRU_WIDTH] = _gated(h * _row_rms_scale(h), gate_a)

    cos = cos_ref[...]
    sin = sin_ref[...]
    q_groups = []
    for g in range(SWA_WIDTH // LANES):
        cols = slice(_P_Q + g * LANES, _P_Q + (g + 1) * LANES)
        qraw = proj_scr[:, cols]
        qn = qraw * _head_rms_scale(qraw, head_mean_ref) * q_gain
        q_groups.append(_rope(qn, cos, sin, partner_ref))
    kraw = proj_scr[:, _P_K:_P_K + SWA_KV_WIDTH]
    kn = kraw * _head_rms_scale(kraw, head_mean_ref) * k_gain
    for name_scr, kv in ((k_scr, _rope(kn, cos, sin, partner_ref)), (v_scr, proj_scr[:, _P_V:_P_V + SWA_KV_WIDTH])):
        name_scr[0, BLOCK:BLOCK + tile, :] = kv.astype(_BF16)
        name_scr[1, BLOCK:BLOCK + tile, :] = pltpu.roll(kv, HEAD_DIM, axis=1).astype(_BF16)

    xq_groups = []
    for g in range(XATTN_WIDTH // LANES):
        cols = slice(_P_XQ + g * LANES, _P_XQ + (g + 1) * LANES)
        xraw = proj_scr[:, cols]
        xq_groups.append(xraw * _head_rms_scale(xraw, head_mean_ref))

    lane = lax.broadcasted_iota(jnp.int32, (BLOCK, LANES), 1)
    low_half = lane < HEAD_DIM
    half_masks = (low_half, lane >= HEAD_DIM)
    qi = lax.broadcasted_iota(jnp.int32, (BLOCK, BLOCK), 0)
    kj = lax.broadcasted_iota(jnp.int32, (BLOCK, BLOCK), 1)
    own_block = kj <= qi

    ones_block = jnp.ones((2 * BLOCK, LANES), _BF16)
    for blk in range(n_blocks):
        rows = slice(blk * BLOCK, (blk + 1) * BLOCK)
        o_head = [None] * SWA_Q_HEADS
        for kv_copy, heads in enumerate(_HEAD_PAIRS):
            kk = k_scr[kv_copy, blk * BLOCK:(blk + 2) * BLOCK, :]
            vv = v_scr[kv_copy, blk * BLOCK:(blk + 2) * BLOCK, :]
            qm = jnp.concatenate([
                jnp.where(half_masks[_HEAD_SLOTS[head][1]], q_groups[_HEAD_SLOTS[head][0]][rows, :], 0.0).astype(_BF16)
                for head in heads], axis=0)
            sc_pair = _dot_nt(qm, kk)
            p_pair, dens = [], []
            for i, head in enumerate(heads):
                sc_both = sc_pair[i * BLOCK:(i + 1) * BLOCK, :]
                sc_prev = sc_both[:, :BLOCK]
                if blk == 0:
                    sc_prev = jnp.where(s > 0, sc_prev, NEG_INF)
                sc = jnp.where(own_block, sc_both[:, BLOCK:], sc_prev)
                sink = sinks_ref[layer, head] * LOG2_E
                m = jnp.maximum(jnp.max(sc, axis=-1, keepdims=True), sink)
                p = jnp.exp2(sc - m)
                dens.append(jnp.exp2(sink - m))
                p_pair.append(jnp.concatenate(
                    [jnp.where(own_block, 0.0, p), jnp.where(own_block, p, 0.0)], axis=-1).astype(_BF16))
            o_pair = _dot(jnp.concatenate(p_pair, axis=0), jnp.concatenate([vv, ones_block], axis=-1))
            for i, head in enumerate(heads):
                o_rows = o_pair[i * BLOCK:(i + 1) * BLOCK, :]
                o_head[head] = o_rows[:, :LANES] * (1.0 / (o_rows[:, LANES:] + dens[i]))
        o_swa = [jnp.where(low_half, o_head[2 * g], o_head[2 * g + 1]) for g in range(SWA_WIDTH // LANES)]
        y_b = jnp.concatenate(o_swa, axis=-1)
        gate_b = proj_scr[rows, _P_GATE_B:_P_GATE_B + SWA_WIDTH]
        y_scr[rows, LRU_WIDTH:LRU_WIDTH + SWA_WIDTH] = _gated(y_b * _row_rms_scale(y_b), gate_b)

        o_mem = []
        for g in range(XATTN_WIDTH // LANES):
            qm = jnp.concatenate([jnp.where(half_masks[hi], xq_groups[g][rows, :], 0.0).astype(_BF16)
                                  for hi in range(LANES // HEAD_DIM)], axis=0)
            sc = _dot_nt(qm, km_scr[:, g * LANES:(g + 1) * LANES])
            m = jnp.max(sc, axis=-1, keepdims=True)
            p = jnp.exp2(sc - m)
            o = _dot(p.astype(_BF16), jnp.concatenate([vm_scr[:, g * LANES:(g + 1) * LANES], ones_block], axis=-1))
            o = o[:, :LANES] * (1.0 / o[:, LANES:])
            o_mem.append(jnp.where(low_half, o[:BLOCK, :], o[BLOCK:, :]))
        y_c = jnp.concatenate(o_mem, axis=-1)
        gate_c = proj_scr[rows, _P_GATE_C:_P_GATE_C + XATTN_WIDTH]
        y_scr[rows, LRU_WIDTH + SWA_WIDTH:] = _gated(y_c * _row_rms_scale(y_c), gate_c)

    o_ref[0] = x + _dot(y_scr[...], w_out_ref[...])


def _rope_tables(seq):
    pos = np.arange(seq, dtype=np.float32)
    inv_freq = np.float32(ROPE_THETA) ** (-(np.arange(0, ROPE_DIM, 2, dtype=np.float32) / np.float32(ROPE_DIM)))
    ang = (pos[:, None] * inv_freq[None, :].astype(np.float32)).astype(np.float32)
    cos, sin = np.cos(ang).astype(np.float32), np.sin(ang).astype(np.float32)
    ones = np.ones((seq, HEAD_DIM - ROPE_DIM), np.float32)
    cos_head = np.concatenate([cos, cos, ones], axis=-1)
    sin_head = np.concatenate([-sin, sin, 0.0 * ones], axis=-1)
    reps = LANES // HEAD_DIM
    return jnp.asarray(np.tile(cos_head, (1, reps))), jnp.asarray(np.tile(sin_head, (1, reps)))


def _head_mean_matrix(width):
    block = np.full((HEAD_DIM, HEAD_DIM), 1.0 / HEAD_DIM, np.float32)
    return jnp.asarray(np.kron(np.eye(width // HEAD_DIM, dtype=np.float32), block), dtype=_BF16)


def _const_spec(shape):
    return pl.BlockSpec(shape, lambda *_: (0,) * len(shape))


def _gate_half_columns():
    scale = np.ones((LRU_WIDTH + _P_END,), np.float32)
    for start, width in ((_P_GATE_A, LRU_WIDTH), (_P_GATE_B, SWA_WIDTH), (_P_GATE_C, XATTN_WIDTH)):
        scale[LRU_WIDTH + start:LRU_WIDTH + start + width] = 0.5
    return scale


def _scale_gate_columns(w, first_col):
    half = _gate_half_columns()
    pieces = []
    for c in range(0, w.shape[1], LANES):
        scale = half[first_col + c:first_col + c + LANES]
        assert (scale == scale[0]).all()
        piece = w[:, c:c + LANES]
        pieces.append(piece if scale[0] == 1.0 else piece * float(scale[0]))
    return jnp.concatenate(pieces, axis=-1)


def _layer(x, mem, p, layer, *, tile):
    B, S, _ = x.shape
    n_tiles = S // tile
    assert tile >= D_MODEL and tile >= D_MIX and 2 * XATTN_WIDTH == LRU_WIDTH
    cos, sin = _rope_tables(S)
    vector_names = ("mem_norm_g", "xk_norm_g", "norm_g", "conv_w", "conv_b", "b_rg", "b_ig",
                    "lru_lambda", "q_norm_g", "k_norm_g", "xq_norm_g", "out_norm_g", "w_rg", "w_ig")
    vectors = [p[name].astype(_F32) for name in vector_names]
    big_weights = [p[name].astype(_F32) for name in ("w_mem_kv", "w_in", "w_out")]
    head_mean = _head_mean_matrix(LANES)
    partner = _rope_partner_matrix()
    operands = [p["sinks"].astype(_F32), x, mem, cos, sin] + vectors + [head_mean, partner] + big_weights
    layer_block = lambda a: pl.BlockSpec((1,) + a.shape[1:], lambda b, s: (layer,) + (0,) * (a.ndim - 1))
    in_specs = [
        pl.BlockSpec(memory_space=pltpu.SMEM),
        pl.BlockSpec((1, tile, D_MODEL), lambda b, s: (b, s, 0)),
        pl.BlockSpec((1, MEM_LEN, D_MODEL), lambda b, s: (b, 0, 0)),
        pl.BlockSpec((tile, LANES), lambda b, s: (s, 0)), pl.BlockSpec((tile, LANES), lambda b, s: (s, 0)),
    ] + [layer_block(v) for v in vectors] + [_const_spec(head_mean.shape), _const_spec(partner.shape)] + [
        pl.BlockSpec(memory_space=pl.ANY) for _ in big_weights]

    return pl.pallas_call(
        functools.partial(_layer_kernel, tile=tile, layer=layer),
        out_shape=jax.ShapeDtypeStruct(x.shape, x.dtype),
        grid=(B, n_tiles),
        in_specs=in_specs,
        out_specs=pl.BlockSpec((1, tile, D_MODEL), lambda b, s: (b, s, 0)),
        scratch_shapes=[
            pltpu.VMEM((tile, _P_END), _F32),
            pltpu.VMEM((LRU_SLABS, SUBLANES * SEG_PITCH, LANES), _F32),
            pltpu.VMEM((LRU_SLABS, SUBLANES * SEG_PITCH, LANES), _F32),
            pltpu.VMEM((1, LRU_WIDTH), _F32),
            pltpu.VMEM(((CONV_WIDTH - 1) * SUBLANES, LRU_WIDTH), _F32),
            pltpu.VMEM((SWA_KV_HEADS, tile + BLOCK, LANES), _BF16),
            pltpu.VMEM((SWA_KV_HEADS, tile + BLOCK, LANES), _BF16),
            pltpu.VMEM((tile, D_MIX), _BF16),
            pltpu.VMEM((MEM_LEN, XATTN_WIDTH), _BF16),
            pltpu.VMEM((MEM_LEN, XATTN_WIDTH), _BF16),
            pltpu.VMEM((D_MODEL, 2 * XATTN_WIDTH), _BF16),
            pltpu.VMEM((D_MODEL, LRU_WIDTH + _P_END), _BF16),
            pltpu.VMEM((LRU_WIDTH // MXU_DIM, MXU_DIM, 2 * MXU_DIM), _BF16),
            pltpu.VMEM((D_MIX, D_MODEL), _BF16),
            pltpu.SemaphoreType.DMA((N_WEIGHT_COPIES,)),
        ],
        compiler_params=pltpu.CompilerParams(
            dimension_semantics=("arbitrary", "arbitrary"), vmem_limit_bytes=VMEM_LIMIT_BYTES),
        name="hymba_layer",
    )(*operands)


def kernel(x, mem, norm_g, mem_norm_g, w_in, conv_w, conv_b, w_rg, b_rg, w_ig, b_ig, lru_lambda, q_norm_g,
           k_norm_g, sinks, w_mem_kv, xq_norm_g, xk_norm_g, out_norm_g, w_out):
    depth = w_in.shape[0]
    tile = TILE
    assert x.shape[1] % tile == 0 and tile % BLOCK == 0 and WINDOW == BLOCK
    assert SWA_KV_HEADS == LANES // HEAD_DIM == 2
    params = dict(
        norm_g=norm_g, w_in=w_in, conv_w=conv_w, conv_b=conv_b, w_rg=w_rg, b_rg=b_rg, w_ig=w_ig, b_ig=b_ig,
        lru_lambda=lru_lambda, q_norm_g=q_norm_g, k_norm_g=k_norm_g, sinks=sinks, xq_norm_g=xq_norm_g,
        out_norm_g=out_norm_g, w_out=w_out, mem_norm_g=mem_norm_g, w_mem_kv=w_mem_kv, xk_norm_g=xk_norm_g)
    h = x
    for l in range(depth):
        h = _layer(h, mem, params, l, tile=tile)
    return h
```
